```python
import math
import jax
import jax.numpy as jnp
from jax import lax
import numpy as np

D_MODEL = 1024
BATCH = 16
SEQ = 4096
DEPTH = 4

GRID_W = 64
CTX_LEN = 256
N_MIXERS = 2
N_ADA = 6
NORM_EPS = 1e-6
GDN_HEADS = 8
GDN_DK = 128
GDN_DV = 128
GDN_CONV = 5
GDN_CHUNK = 64
GDN_QK_W = GDN_HEADS * GDN_DK
GDN_V_W = GDN_HEADS * GDN_DV
GDN_IN_W = 2 * GDN_QK_W + 2 * GDN_V_W + 4 * GDN_HEADS
GMLP_CHUNK = 128
GMLP_GROUPS = 8
GMLP_WIDTH = 2 * D_MODEL
ROWS_PER_CHUNK = GMLP_CHUNK // GRID_W
N_EXPERTS = 32
TOP_K = 4
D_EXPERT = D_MODEL
SWIGLU_ALPHA = 1.702
SWIGLU_LIMIT = 7.0

kernel_name = 'hybrid_gdn_chunkgmlp_moe_prefix_dit'


def _n_layers_of(m):
    return (DEPTH - m + N_MIXERS - 1) // N_MIXERS


def rmsnorm(x, g):
    xf = x.astype(jnp.float32)
    y = xf * lax.rsqrt(jnp.mean(xf * xf, axis=-1, keepdims=True) + NORM_EPS)
    return y.astype(x.dtype) * g


def layernorm(x, g, bias):
    xf = x.astype(jnp.float32)
    mu = jnp.mean(xf, axis=-1, keepdims=True)
    xc = xf - mu
    y = xc * lax.rsqrt(jnp.mean(xc * xc, axis=-1, keepdims=True) + NORM_EPS)
    return y.astype(x.dtype) * g + bias


def l2norm(x):
    return x * lax.rsqrt(jnp.sum(x * x, axis=-1, keepdims=True) + NORM_EPS)


def modulate(h, shift, scale):
    return h * (1.0 + scale) + shift


def centred_dwconv(x, w):
    k, ch = w.shape
    return lax.conv_general_dilated(x, w[:, None, :].astype(x.dtype), window_strides=(1,),
                                    padding=[(k // 2, k // 2)], dimension_numbers=('NWC', 'WIO', 'NWC'),
                                    feature_group_count=ch)


def chunk_gated_delta(q, k, v, g, beta, s0):
    b, h, t, dk = k.shape
    cs = GDN_CHUNK
    n = t // cs
    q = q * (dk ** -0.5)
    k_beta = k * beta[..., None]
    v_beta = v * beta[..., None]
    chunk = lambda a: a.reshape(b, h, n, cs, a.shape[-1])
    q, k, k_beta, v_beta = chunk(q), chunk(k), chunk(k_beta), chunk(v_beta)
    g = jnp.cumsum(g.reshape(b, h, n, cs), axis=-1)
    idx = jnp.arange(cs)
    incl = idx[:, None] >= idx[None, :]
    strict = idx[:, None] > idx[None, :]
    decay = jnp.exp(jnp.where(incl, g[..., :, None] - g[..., None, :], -jnp.inf))
    lmat = jnp.where(strict, jnp.einsum('bhnid,bhnjd->bhnij', k_beta, k) * decay, 0.0)
    eye = jnp.eye(cs, dtype=lmat.dtype)
    t_inv = lax.linalg.triangular_solve(eye + lmat, jnp.broadcast_to(eye, lmat.shape),
                                        left_side=True, lower=True)
    u = t_inv @ v_beta
    w = t_inv @ (k_beta * jnp.exp(g)[..., None])
    a_intra = jnp.einsum('bhnid,bhnjd->bhnij', q, k) * decay
    q_dec = q * jnp.exp(g)[..., None]
    g_last = g[..., -1]
    k_tail = k * jnp.exp(g_last[..., None] - g)[..., None]

    def step(s, xs):
        q_i, k_i, u_i, w_i, a_i, d_i = xs
        v_new = u_i - w_i @ s
        o_i = q_i @ s + a_i @ v_new
        s = s * d_i[..., None, None] + jnp.swapaxes(k_i, -1, -2) @ v_new
        return s, o_i

    xs = tuple(jnp.moveaxis(a, 2, 0) for a in (q_dec, k_tail, u, w, a_intra, jnp.exp(g_last)))
    s_final, o = lax.scan(step, s0, xs)
    o = jnp.moveaxis(o, 0, 2).reshape(b, h, t, -1)
    return o, s_final


def bidir_gated_delta(q, k, v, g, beta, s0_fwd, s0_bwd):
    o_f, s_f = chunk_gated_delta(q, k, v, g[0], beta[0], s0_fwd)
    rev = lambda a: jnp.flip(a, axis=2)
    o_b, s_b = chunk_gated_delta(rev(q), rev(k), rev(v), rev(g[1]), rev(beta[1]), s0_bwd)
    return o_f + rev(o_b), s_f, s_b


def gdn_features(h, w_in, conv_w, a_log, dt_bias):
    b, t, _ = h.shape
    proj = h @ w_in
    o1 = 2 * GDN_QK_W + GDN_V_W
    qkv = jax.nn.silu(centred_dwconv(proj[..., :o1], conv_w))

    def heads(a, dh):
        return a.reshape(b, t, GDN_HEADS, dh).transpose(0, 2, 1, 3).astype(jnp.float32)

    q = l2norm(heads(qkv[..., :GDN_QK_W], GDN_DK))
    k = l2norm(heads(qkv[..., GDN_QK_W:2 * GDN_QK_W], GDN_DK))
    v = heads(qkv[..., 2 * GDN_QK_W:], GDN_DV)
    z = proj[..., o1:o1 + GDN_V_W]
    o2 = o1 + GDN_V_W
    a = proj[..., o2:o2 + 2 * GDN_HEADS].astype(jnp.float32).reshape(b, t, 2, GDN_HEADS)
    bb = proj[..., o2 + 2 * GDN_HEADS:].astype(jnp.float32).reshape(b, t, 2, GDN_HEADS)
    g = -jnp.exp(a_log.astype(jnp.float32)) * jax.nn.softplus(a + dt_bias.astype(jnp.float32))
    beta = jax.nn.sigmoid(bb)
    to_dir = lambda x: x.transpose(2, 0, 3, 1)
    return q, k, v, z, to_dir(g), to_dir(beta)


def gdn_output(o, z, norm_g, w_out):
    b, hh, t, dv = o.shape
    o = jnp.swapaxes(o, 1, 2)
    o = o * lax.rsqrt(jnp.mean(o * o, axis=-1, keepdims=True) + NORM_EPS)
    o = o * norm_g.astype(jnp.float32) * jax.nn.silu(z.reshape(b, t, hh, dv).astype(jnp.float32))
    return o.reshape(b, t, hh * dv).astype(w_out.dtype) @ w_out


def chunk_gmlp(h, n_chunks, w_in, b_in, ln_g, ln_b, w_s, b_s, w_out):
    b, t, _ = h.shape
    zz = jax.nn.gelu(h @ w_in + b_in, approximate=False)
    u, v = zz[..., :GMLP_WIDTH], zz[..., GMLP_WIDTH:]
    v = layernorm(v, ln_g, ln_b)
    v = v.reshape(b, n_chunks, GMLP_CHUNK, GMLP_GROUPS, GMLP_WIDTH // GMLP_GROUPS)
    s = jnp.einsum('gij,bnjgc->bnigc', w_s, v) + b_s.T[:, :, None]
    return (u * s.reshape(b, t, GMLP_WIDTH)) @ w_out


def clamped_swiglu_expert(h, w_gu, b_gu, w_dn, b_dn):
    gu = h @ w_gu + b_gu
    gate = jnp.minimum(gu[..., :D_EXPERT], SWIGLU_LIMIT)
    up = jnp.clip(gu[..., D_EXPERT:], -SWIGLU_LIMIT, SWIGLU_LIMIT)
    glu = gate * jax.nn.sigmoid(gate * SWIGLU_ALPHA)
    return ((up + 1.0) * glu) @ w_dn + b_dn


def moe(h, w_r, b_r, w_gu, b_gu, w_dn, b_dn):
    logits = (h @ w_r + b_r).astype(jnp.float32)
    top_v, top_i = lax.top_k(logits, TOP_K)
    probs = jax.nn.softmax(top_v, axis=-1)
    comb = jnp.einsum('nke,nk->ne', jax.nn.one_hot(top_i, N_EXPERTS, dtype=jnp.float32), probs).astype(h.dtype)
    y = jnp.zeros_like(h)
    for e in range(N_EXPERTS):
        y = y + comb[:, e:e + 1] * clamped_swiglu_expert(h, w_gu[e], b_gu[e], w_dn[e], b_dn[e])
    return y


def setup_inputs(seed: int = 0) -> dict:
    key = jax.random.key(seed)
    ks = iter(jax.random.split(key, 40))
    f32 = jnp.float32
    D = D_MODEL
    na, nb = _n_layers_of(0), _n_layers_of(1)

    def nrm(shape, scale):
        return jax.random.normal(next(ks), shape, f32) * scale

    def gain(shape):
        return 1.0 + nrm(shape, 0.05)

    x = nrm((BATCH, SEQ, D), 1.0)
    c = nrm((BATCH, D), 1.0)
    ctx = nrm((BATCH, CTX_LEN, D), 1.0)
    c_ctx = nrm((D,), 1.0)
    ada_w = nrm((DEPTH, D, N_ADA * D), 0.5 * D ** -0.5)
    ada_b = nrm((DEPTH, N_ADA * D), 0.02)
    norm1_g = gain((DEPTH, D))
    norm2_g = gain((DEPTH, D))
    gdn_w_in = nrm((na, D, GDN_IN_W), D ** -0.5)
    gdn_conv_w = nrm((na, GDN_CONV, 2 * GDN_QK_W + GDN_V_W), GDN_CONV ** -0.5)
    gdn_a_log = jnp.log(jax.random.uniform(next(ks), (na, 2, GDN_HEADS), f32, 1.0, 16.0))
    dt = jnp.exp(jax.random.uniform(next(ks), (na, 2, GDN_HEADS), f32, math.log(1e-3), math.log(1e-1)))
    gdn_dt_bias = dt + jnp.log(-jnp.expm1(-dt))
    gdn_norm_g = gain((na, GDN_DV))
    gdn_w_out = nrm((na, GDN_V_W, D), GDN_V_W ** -0.5)
    gmlp_w_in = nrm((nb, D, 2 * GMLP_WIDTH), D ** -0.5)
    gmlp_b_in = nrm((nb, 2 * GMLP_WIDTH), 0.02)
    gmlp_ln_g = gain((nb, GMLP_WIDTH))
    gmlp_ln_b = nrm((nb, GMLP_WIDTH), 0.02)
    gmlp_w_s = nrm((nb, GMLP_GROUPS, GMLP_CHUNK, GMLP_CHUNK), 0.5 * GMLP_CHUNK ** -0.5)
    gmlp_b_s = gain((nb, GMLP_GROUPS, GMLP_CHUNK))
    gmlp_w_out = nrm((nb, GMLP_WIDTH, D), GMLP_WIDTH ** -0.5)
    router_w = nrm((DEPTH, D, N_EXPERTS), D ** -0.5)
    router_b = nrm((DEPTH, N_EXPERTS), 0.01)
    moe_w_gu = nrm((DEPTH, N_EXPERTS, D, 2 * D_EXPERT), D ** -0.5)
    moe_b_gu = nrm((DEPTH, N_EXPERTS, 2 * D_EXPERT), 0.01)
    moe_w_dn = nrm((DEPTH, N_EXPERTS, D_EXPERT, D), D_EXPERT ** -0.5)
    moe_b_dn = nrm((DEPTH, N_EXPERTS, D), 0.01)
    final_g = gain((D,))
    return {'x': x, 'c': c, 'ctx': ctx, 'c_ctx': c_ctx, 'ada_w': ada_w, 'ada_b': ada_b,
            'norm1_g': norm1_g, 'norm2_g': norm2_g, 'gdn_w_in': gdn_w_in, 'gdn_conv_w': gdn_conv_w,
            'gdn_a_log': gdn_a_log, 'gdn_dt_bias': gdn_dt_bias, 'gdn_norm_g': gdn_norm_g, 'gdn_w_out': gdn_w_out,
            'gmlp_w_in': gmlp_w_in, 'gmlp_b_in': gmlp_b_in, 'gmlp_ln_g': gmlp_ln_g, 'gmlp_ln_b': gmlp_ln_b,
            'gmlp_w_s': gmlp_w_s, 'gmlp_b_s': gmlp_b_s, 'gmlp_w_out': gmlp_w_out,
            'router_w': router_w, 'router_b': router_b, 'moe_w_gu': moe_w_gu, 'moe_b_gu': moe_b_gu,
            'moe_w_dn': moe_w_dn, 'moe_b_dn': moe_b_dn, 'final_g': final_g}


def reference(x, c, ctx, c_ctx, ada_w, ada_b, norm1_g, norm2_g, gdn_w_in, gdn_conv_w, gdn_a_log, gdn_dt_bias,
              gdn_norm_g, gdn_w_out, gmlp_w_in, gmlp_b_in, gmlp_ln_g, gmlp_ln_b, gmlp_w_s, gmlp_b_s, gmlp_w_out,
              router_w, router_b, moe_w_gu, moe_b_gu, moe_w_dn, moe_b_dn, final_g):
    b, t, d = x.shape
    rows = t // GRID_W
    n_lat_chunks = rows // ROWS_PER_CHUNK
    n_ctx_chunks = ctx.shape[1] // GMLP_CHUNK
    last_ctx_reader = max([i for i in range(DEPTH) if i % N_MIXERS == 0])
    silu_c = jax.nn.silu(c)
    silu_cc = jax.nn.silu(c_ctx)
    zero_state = jnp.zeros((b, GDN_HEADS, GDN_DK, GDN_DV), jnp.float32)
    for i in range(DEPTH):
        j = i // N_MIXERS
        ctx_in = i <= last_ctx_reader
        ctx_out = i < last_ctx_reader
        mx = [m[:, None, :] for m in jnp.split(silu_c @ ada_w[i] + ada_b[i], N_ADA, axis=-1)]
        hx = modulate(rmsnorm(x, norm1_g[i]), mx[0], mx[1])
        if ctx_in:
            mc = jnp.split(silu_cc @ ada_w[i] + ada_b[i], N_ADA, axis=-1)
            hc = modulate(rmsnorm(ctx, norm1_g[i]), mc[0], mc[1])
        if i % N_MIXERS == 0:
            p = (gdn_w_in[j], gdn_conv_w[j], gdn_a_log[j], gdn_dt_bias[j])
            qc, kc, vc, zc, gc, bc = gdn_features(hc, *p)
            o_c, s_f, s_b = bidir_gated_delta(qc, kc, vc, gc, bc, zero_state, zero_state)
            qx, kx, vx, zx, gx, bx = gdn_features(hx, *p)
            o_x, _, _ = bidir_gated_delta(qx, kx, vx, gx, bx, s_f, s_b)
            yx = gdn_output(o_x, zx, gdn_norm_g[j], gdn_w_out[j])
            if ctx_out:
                yc = gdn_output(o_c, zc, gdn_norm_g[j], gdn_w_out[j])
        else:
            gp = (gmlp_w_in[j], gmlp_b_in[j], gmlp_ln_g[j], gmlp_ln_b[j], gmlp_w_s[j], gmlp_b_s[j], gmlp_w_out[j])
            yx = chunk_gmlp(hx, n_lat_chunks, *gp)
            if ctx_out:
                yc = chunk_gmlp(hc, n_ctx_chunks, *gp)
        x = x + mx[2] * yx
        mp = (router_w[i], router_b[i], moe_w_gu[i], moe_b_gu[i], moe_w_dn[i], moe_b_dn[i])
        hx2 = modulate(rmsnorm(x, norm2_g[i]), mx[3], mx[4]).reshape(-1, d)
        if ctx_out:
            ctx = ctx + mc[2] * yc
            hc2 = modulate(rmsnorm(ctx, norm2_g[i]), mc[3], mc[4]).reshape(-1, d)
            y2 = moe(jnp.concatenate([hx2, hc2], axis=0), *mp)
            x = x + mx[5] * y2[:b * t].reshape(x.shape)
            ctx = ctx + mc[5] * y2[b * t:].reshape(ctx.shape)
        else:
            x = x + mx[5] * moe(hx2, *mp).reshape(x.shape)
    return rmsnorm(x, final_g)
```

```python
import functools

import jax
import jax.numpy as jnp
from jax import lax
from jax.experimental import pallas as pl
from jax.experimental.pallas import tpu as pltpu

F32 = jnp.float32
BF16 = jnp.bfloat16

NORM_EPS = 1e-6
N_ADA = 6
GDN_HEADS = 8
GDN_DK = 128
GDN_CONV = 5
GDN_CHUNK = 64
GMLP_CHUNK = 128
GMLP_GROUPS = 8
TOP_K = 4
SWIGLU_ALPHA = 1.702
SWIGLU_LIMIT = 7.0

LANES = 128
TOK_TILE = 256
MOE_TILE = 512
INV_BLOCK = 16
VMEM_LIMIT = 56 * 1024 * 1024


def _cparams(sem):
    return pltpu.CompilerParams(dimension_semantics=sem, vmem_limit_bytes=VMEM_LIMIT)


def _dot(a, b):
    return jnp.dot(a.astype(BF16), b.astype(BF16), preferred_element_type=F32)


def _dot_nt(a, b):
    return lax.dot_general(a.astype(BF16), b.astype(BF16), (((1,), (1,)), ((), ())),
                           preferred_element_type=F32)


def _dot_tn(a, b):
    return lax.dot_general(a.astype(BF16), b.astype(BF16), (((0,), (0,)), ((), ())),
                           preferred_element_type=F32)


def _split(a):
    hi = a.astype(BF16)
    lo = (a - hi.astype(F32)).astype(BF16)
    return hi, lo


def _dot3(a, b):
    ah, al = _split(a)
    bh, bl = _split(b)
    d = lambda x, y: jnp.dot(x, y, preferred_element_type=F32)
    return d(ah, bh) + (d(ah, bl) + d(al, bh))


def _rms_scale(xf):
    return xf * lax.rsqrt(jnp.mean(xf * xf, axis=-1, keepdims=True) + NORM_EPS)


def _silu(x):
    return x * jax.nn.sigmoid(x)


def _mod_part(m, k, d):
    return m[:, k * d:(k + 1) * d]


def _ada_kernel(cc_ref, w_ref, b_ref, o_ref):
    s = _silu(cc_ref[...])
    o_ref[...] = jnp.dot(s, w_ref[...], preferred_element_type=F32,
                         precision=lax.Precision.HIGHEST) + b_ref[...]


def _ada_mods(cc, ada_w, ada_b):
    depth, d, n = ada_w.shape
    r = cc.shape[0]
    tn = n // 4
    return pl.pallas_call(
        _ada_kernel,
        grid=(depth, n // tn),
        in_specs=[pl.BlockSpec((r, d), lambda l, j: (0, 0)),
                  pl.BlockSpec((None, d, tn), lambda l, j: (l, 0, j)),
                  pl.BlockSpec((None, 1, tn), lambda l, j: (l, 0, j))],
        out_specs=pl.BlockSpec((None, r, tn), lambda l, j: (l, 0, j)),
        out_shape=jax.ShapeDtypeStruct((depth, r, n), F32),
        compiler_params=_cparams(("parallel", "parallel")),
        name="ada_mods",
    )(cc, ada_w, ada_b.reshape(depth, 1, n))


def _route_epilogue(x_new, m, g2, wr_hi, wr_lo, br, tri, h2_ref, route_ref, cnt_ref):
    d = x_new.shape[-1]
    tm = x_new.shape[0]
    h2 = (_rms_scale(x_new) * g2) * (1.0 + _mod_part(m, 4, d)) + _mod_part(m, 3, d)
    h2_ref[...] = h2
    hh, hl = _split(h2)
    dd = lambda a, b: jnp.dot(a, b, preferred_element_type=F32)
    logits = dd(hh, wr_hi) + (dd(hh, wr_lo) + dd(hl, wr_hi)) + br
    lane = lax.broadcasted_iota(jnp.int32, (tm, LANES), 1).astype(F32)
    vals, idxs = [], []
    l = logits
    for _ in range(TOP_K):
        mx = jnp.max(l, axis=1, keepdims=True)
        ix = jnp.min(jnp.where(l == mx, lane, float(LANES)), axis=1, keepdims=True)
        vals.append(mx)
        idxs.append(ix)
        l = jnp.where(lane == ix, -jnp.inf, l)
    es = [jnp.exp(v - vals[0]) for v in vals]
    den = es[0] + es[1] + es[2] + es[3]
    hots = [(lane == ix).astype(F32) for ix in idxs]
    oh4 = hots[0] + hots[1] + hots[2] + hots[3]
    prefix = jnp.dot(tri, oh4.astype(BF16), preferred_element_type=F32)
    route = jnp.zeros((tm, LANES), F32)
    for k in range(TOP_K):
        rank = jnp.sum(hots[k] * prefix, axis=1, keepdims=True)
        route = jnp.where(lane == float(k), idxs[k], route)
        route = jnp.where(lane == float(TOP_K + k), es[k] / den, route)
        route = jnp.where(lane == float(2 * TOP_K + k), rank, route)
    route_ref[...] = route
    cnt_ref[...] = jnp.sum(oh4, axis=0, keepdims=True)


def _strict_lower(n):
    i = lax.broadcasted_iota(jnp.int32, (n, n), 0)
    j = lax.broadcasted_iota(jnp.int32, (n, n), 1)
    return (j < i).astype(BF16)


def _gdn_in_kernel(x_ref, m_ref, g1_ref, w_ref, wab_ref, ab_ref, qkv_ref, z_ref, gb_ref, *, qkv_w):
    d = x_ref.shape[-1]
    m = m_ref[...]
    h = (_rms_scale(x_ref[...]) * g1_ref[...]) * (1.0 + _mod_part(m, 1, d)) + _mod_part(m, 0, d)
    hb = h.astype(BF16)
    p = jnp.dot(hb, w_ref[...], preferred_element_type=F32)
    qkv_ref[...] = p[:, :qkv_w].astype(BF16)
    z_ref[...] = p[:, qkv_w:].astype(BF16)
    ab = jnp.dot(hb, wab_ref[...], preferred_element_type=F32)
    neg_a = ab_ref[0:1, :]
    dt_b = ab_ref[1:2, :]
    xs = ab + dt_b
    softplus = jnp.maximum(xs, 0.0) + jnp.log(1.0 + jnp.exp(-jnp.abs(xs)))
    lane = lax.broadcasted_iota(jnp.int32, ab.shape, 1)
    gb_ref[...] = jnp.where(lane < 2 * GDN_HEADS, neg_a * softplus, jax.nn.sigmoid(ab))


def _gdn_conv_kernel(x_ref, w_ref, o_ref, pad_ref, *, n_ctx, n_lat, qk_blocks):
    tile = TOK_TILE
    zeros8 = jnp.zeros((8, LANES), F32)
    pad_ref[0:8, :] = zeros8
    pad_ref[8 + n_ctx:16 + n_ctx, :] = zeros8
    pad_ref[16 + n_ctx + n_lat:24 + n_ctx + n_lat, :] = zeros8
    for r in range(0, n_ctx + n_lat, tile):
        base = 8 + r if r < n_ctx else 16 + r
        pad_ref[base:base + tile, :] = x_ref[r:r + tile, :].astype(F32)
    w = w_ref[...]
    is_qk = pl.program_id(1) < qk_blocks
    half = GDN_CONV // 2
    for r in range(0, n_ctx + n_lat, tile):
        base = 8 + r if r < n_ctx else 16 + r
        acc = jnp.zeros((tile, LANES), F32)
        for k in range(GDN_CONV):
            s = base + k - half
            acc = acc + pad_ref[s:s + tile, :] * w[k:k + 1, :]
        y = _silu(acc)
        yn = y * lax.rsqrt(jnp.sum(y * y, axis=-1, keepdims=True) + NORM_EPS)
        o_ref[r:r + tile, :] = jnp.where(is_qk, yn, y).astype(BF16)


def _unit_tri_inverse(lmat, bi, bj):
    n = lmat.shape[0]
    eye = (lax.broadcasted_iota(jnp.int32, (n, n), 0) == lax.broadcasted_iota(jnp.int32, (n, n), 1)).astype(F32)
    ld = jnp.where(bi == bj, lmat, 0.0)
    x = eye - ld
    p = _dot3(ld, ld)
    width = 2
    while True:
        x = x + _dot3(x, p)
        width *= 2
        if width >= INV_BLOCK:
            break
        p = _dot3(p, p)
    span = 2
    while INV_BLOCK * span <= n:
        half = span // 2
        c = jnp.where(((bi // span) == (bj // span)) & ((bi // half) != (bj // half)), lmat, 0.0)
        x = x - _dot3(_dot3(x, c), x)
        span *= 2
    return x


def _delta_chunk(q, k, v, qk, kk, g_col, g_row, beta, s, reverse, scale):
    cs = q.shape[0]
    ii = lax.broadcasted_iota(jnp.int32, (cs, cs), 0)
    jj = lax.broadcasted_iota(jnp.int32, (cs, cs), 1)
    if reverse:
        incl, strict, incl_t = ii <= jj, ii < jj, ii >= jj
    else:
        incl, strict, incl_t = ii >= jj, ii > jj, ii <= jj
    gc_col = jnp.sum(jnp.where(incl, g_row, 0.0), axis=1, keepdims=True)
    gc_row = jnp.sum(jnp.where(incl_t, g_col, 0.0), axis=0, keepdims=True)
    g_tot = jnp.sum(g_row, axis=1, keepdims=True)
    decay = jnp.exp(jnp.where(incl, gc_col - gc_row, -jnp.inf))
    lmat = jnp.where(strict, beta * kk * decay, 0.0)
    t_inv = _unit_tri_inverse(lmat, ii // INV_BLOCK, jj // INV_BLOCK)
    eg = jnp.exp(gc_col)
    rhs = jnp.concatenate([v * beta, k * (beta * eg)], axis=1)
    uw = _dot3(t_inv, rhs)
    u, w = uw[:, :v.shape[1]], uw[:, v.shape[1]:]
    a_intra = qk * (scale * decay)
    q_dec = q * (scale * eg)
    k_tail = k * jnp.exp(g_tot - gc_col)
    wq = _dot(jnp.concatenate([w, q_dec], axis=0), s)
    v_new = u - wq[:cs]
    o = wq[cs:] + _dot(a_intra, v_new)
    s_new = s * jnp.exp(g_tot) + _dot_tn(k_tail, v_new)
    return o, s_new


def _gdn_scan_kernel(qf_ref, kf_ref, vf_ref, gbf_ref, gtf_ref, qb_ref, kb_ref, vb_ref, gbb_ref, gtb_ref,
                     of_ref, ob_ref, sf_ref, sb_ref, *, scale):
    h = pl.program_id(1)

    @pl.when(pl.program_id(2) == 0)
    def _():
        sf_ref[...] = jnp.zeros_like(sf_ref)
        sb_ref[...] = jnp.zeros_like(sb_ref)

    tm = qf_ref.shape[0]
    lane = lax.broadcasted_iota(jnp.int32, (tm, LANES), 1)

    def col(gb, idx):
        return jnp.sum(jnp.where(lane == idx, gb, 0.0), axis=1, keepdims=True)

    gbf = gbf_ref[...]
    gbb = gbb_ref[...]
    cols = {False: (col(gbf, h), col(gbf, 2 * GDN_HEADS + h)),
            True: (col(gbb, GDN_HEADS + h), col(gbb, 3 * GDN_HEADS + h))}
    rows = {False: gtf_ref[pl.ds(h, 1), :], True: gtb_ref[pl.ds(GDN_HEADS + h, 1), :]}
    refs = {False: (qf_ref, kf_ref, vf_ref, of_ref, sf_ref), True: (qb_ref, kb_ref, vb_ref, ob_ref, sb_ref)}
    n_chunks = tm // GDN_CHUNK
    for reverse in (False, True):
        q_ref, k_ref, v_ref, o_ref, s_ref = refs[reverse]
        g_col_all, beta_all = cols[reverse]
        g_row_all = rows[reverse]
        s = s_ref[...]
        order = range(n_chunks - 1, -1, -1) if reverse else range(n_chunks)
        for c in order:
            sl = slice(c * GDN_CHUNK, (c + 1) * GDN_CHUNK)
            q = q_ref[sl, :].astype(F32)
            k = k_ref[sl, :].astype(F32)
            v = v_ref[sl, :].astype(F32)
            qkk = _dot_nt(jnp.concatenate([q, k], axis=0), k)
            o, s = _delta_chunk(q, k, v, qkk[:GDN_CHUNK], qkk[GDN_CHUNK:], g_col_all[sl, :], g_row_all[:, sl],
                                beta_all[sl, :], s, reverse, scale)
            o_ref[sl, :] = o.astype(o_ref.dtype)
        s_ref[...] = s


def _gdn_out_kernel(of_ref, ob_ref, z_ref, x_ref, m_ref, ng_ref, w_ref, g2_ref, wrh_ref, wrl_ref, br_ref,
                    xo_ref, h2_ref, route_ref, cnt_ref):
    d = x_ref.shape[-1]
    m = m_ref[...]
    o = of_ref[...].astype(F32) + ob_ref[...].astype(F32)
    z = z_ref[...].astype(F32)
    ng = ng_ref[...]
    parts = []
    for hd in range(o.shape[1] // LANES):
        sl = slice(hd * LANES, (hd + 1) * LANES)
        parts.append(_rms_scale(o[:, sl]) * ng * _silu(z[:, sl]))
    y = jnp.dot(jnp.concatenate(parts, axis=1).astype(BF16), w_ref[...], preferred_element_type=F32)
    x_new = x_ref[...] + _mod_part(m, 2, d) * y
    xo_ref[...] = x_new
    _route_epilogue(x_new, m, g2_ref[...], wrh_ref[...], wrl_ref[...], br_ref[...],
                    _strict_lower(x_new.shape[0]), h2_ref, route_ref, cnt_ref)


def _gmlp_kernel(x_ref, m_ref, g1_ref, win_ref, bin_ref, lng_ref, lnb_ref, ws_ref, bs_ref, wout_ref,
                 g2_ref, wrh_ref, wrl_ref, br_ref, xo_ref, h2_ref, route_ref, cnt_ref):
    d = x_ref.shape[-1]
    tm = x_ref.shape[0]
    m = m_ref[...]
    x = x_ref[...]
    h = (_rms_scale(x) * g1_ref[...]) * (1.0 + _mod_part(m, 1, d)) + _mod_part(m, 0, d)
    zz = jnp.dot(h.astype(BF16), win_ref[...], preferred_element_type=F32) + bin_ref[...]
    zz = 0.5 * zz * (1.0 + lax.erf(zz * (2.0 ** -0.5)))
    width = zz.shape[1] // 2
    u, v = zz[:, :width], zz[:, width:]
    mu = jnp.mean(v, axis=-1, keepdims=True)
    vc = v - mu
    v = (vc * lax.rsqrt(jnp.mean(vc * vc, axis=-1, keepdims=True) + NORM_EPS)) * lng_ref[...] + lnb_ref[...]
    vb = v.astype(BF16)
    gw = width // GMLP_GROUPS
    rows = []
    for c in range(tm // GMLP_CHUNK):
        rs = slice(c * GMLP_CHUNK, (c + 1) * GMLP_CHUNK)
        cols = []
        for g in range(GMLP_GROUPS):
            sg = jnp.dot(ws_ref[g], vb[rs, g * gw:(g + 1) * gw], preferred_element_type=F32)
            cols.append(sg + bs_ref[:, g:g + 1])
        rows.append(jnp.concatenate(cols, axis=1))
    s = jnp.concatenate(rows, axis=0)
    y = jnp.dot((u * s).astype(BF16), wout_ref[...], preferred_element_type=F32)
    x_new = x + _mod_part(m, 2, d) * y
    xo_ref[...] = x_new
    _route_epilogue(x_new, m, g2_ref[...], wrh_ref[...], wrl_ref[...], br_ref[...],
                    _strict_lower(tm), h2_ref, route_ref, cnt_ref)


def _dispatch_kernel(slot_ref, h_ref, init_ref, o_ref, sem):
    del init_ref
    n = h_ref.shape[0]

    def row_copy(r, k):
        return pltpu.make_async_copy(h_ref.at[pl.ds(r, 1), :],
                                     o_ref.at[pl.ds(slot_ref[0, r * TOP_K + k], 1), :], sem)

    def start(r, c):
        for k in range(TOP_K):
            row_copy(r, k).start()
        return c

    def wait(r, c):
        for k in range(TOP_K):
            row_copy(r, k).wait()
        return c

    lax.fori_loop(0, n, start, 0)
    lax.fori_loop(0, n, wait, 0)


def _expert_kernel(te_ref, nt_ref, x_ref, wgu_ref, bgu_ref, wdn_ref, bdn_ref, o_ref):
    del te_ref

    @pl.when(pl.program_id(0) < nt_ref[0])
    def _():
        de = wdn_ref.shape[0]
        gu = jnp.dot(x_ref[...].astype(BF16), wgu_ref[...], preferred_element_type=F32) + bgu_ref[...]
        gate = jnp.minimum(gu[:, :de], SWIGLU_LIMIT)
        up = jnp.clip(gu[:, de:], -SWIGLU_LIMIT, SWIGLU_LIMIT)
        glu = gate * jax.nn.sigmoid(gate * SWIGLU_ALPHA)
        act = ((up + 1.0) * glu).astype(BF16)
        o_ref[...] = jnp.dot(act, wdn_ref[...], preferred_element_type=F32) + bdn_ref[...]

    @pl.when(pl.program_id(0) >= nt_ref[0])
    def _():
        o_ref[...] = jnp.zeros_like(o_ref)


def _combine_kernel(slot_ref, x_ref, route_ref, m_ref, y_ref, o_ref, buf_ref, sem):
    n, d = x_ref.shape

    def row_copy(r, k):
        return pltpu.make_async_copy(y_ref.at[pl.ds(slot_ref[0, r * TOP_K + k], 1), :],
                                     buf_ref.at[k, pl.ds(r, 1), :], sem)

    def start(r, c):
        for k in range(TOP_K):
            row_copy(r, k).start()
        return c

    def wait(r, c):
        for k in range(TOP_K):
            row_copy(r, k).wait()
        return c

    lax.fori_loop(0, n, start, 0)
    lax.fori_loop(0, n, wait, 0)
    route = route_ref[...]
    acc = jnp.zeros((n, d), F32)
    for k in range(TOP_K):
        acc = acc + route[:, TOP_K + k:TOP_K + k + 1] * buf_ref[k]
    o_ref[...] = x_ref[...] + _mod_part(m_ref[...], 5, d) * acc


def _final_kernel(x_ref, g_ref, o_ref):
    o_ref[...] = _rms_scale(x_ref[...]) * g_ref[...]


def _moe(xs, h2, route, counts, mods_l, mod_map, w_gu, b_gu, w_dn, b_dn, n_bt):
    b, s, d = xs.shape
    n = b * s
    n_exp = w_gu.shape[0]
    nt_tok = n // TOK_TILE
    max_tiles = (n * TOP_K) // MOE_TILE + n_exp
    route2 = route.reshape(n, LANES)
    top_i = route2[:, :TOP_K].astype(jnp.int32)
    rank = route2[:, 2 * TOP_K:3 * TOP_K].astype(jnp.int32)
    cnt = counts.reshape(nt_tok, LANES)[:, :n_exp].astype(jnp.int32)
    totals = jnp.sum(cnt, axis=0)
    tiles_e = (totals + MOE_TILE - 1) // MOE_TILE
    tile_end = jnp.cumsum(tiles_e)
    starts = (tile_end - tiles_e) * MOE_TILE
    tile_off = jnp.cumsum(cnt, axis=0) - cnt
    tok_tile = jnp.arange(n, dtype=jnp.int32) // TOK_TILE
    slot = starts[top_i] + jnp.take_along_axis(tile_off[tok_tile], top_i, axis=1) + rank
    slot3 = slot.reshape(nt_tok, 1, TOK_TILE * TOP_K)
    tile_expert = jnp.minimum(jnp.searchsorted(tile_end, jnp.arange(max_tiles, dtype=jnp.int32), side="right"),
                              n_exp - 1).astype(jnp.int32)
    n_tiles = tile_end[-1:].astype(jnp.int32)

    h2f = h2.reshape(n, d)
    hs = pl.pallas_call(
        _dispatch_kernel,
        grid=(nt_tok,),
        in_specs=[pl.BlockSpec((None, 1, TOK_TILE * TOP_K), lambda i: (i, 0, 0), memory_space=pltpu.SMEM),
                  pl.BlockSpec((TOK_TILE, d), lambda i: (i, 0)),
                  pl.BlockSpec(memory_space=pl.ANY)],
        out_specs=pl.BlockSpec(memory_space=pl.ANY),
        out_shape=jax.ShapeDtypeStruct((max_tiles * MOE_TILE, d), F32),
        scratch_shapes=[pltpu.SemaphoreType.DMA],
        input_output_aliases={2: 0},
        compiler_params=_cparams(("arbitrary",)),
        name="moe_dispatch",
    )(slot3, h2f, jnp.zeros((max_tiles * MOE_TILE, d), F32))

    de = w_dn.shape[1]
    ys = pl.pallas_call(
        _expert_kernel,
        grid_spec=pltpu.PrefetchScalarGridSpec(
            num_scalar_prefetch=2,
            grid=(max_tiles,),
            in_specs=[pl.BlockSpec((MOE_TILE, d), lambda j, te, nt: (j, 0)),
                      pl.BlockSpec((None, d, 2 * de), lambda j, te, nt: (te[j], 0, 0)),
                      pl.BlockSpec((None, 1, 2 * de), lambda j, te, nt: (te[j], 0, 0)),
                      pl.BlockSpec((None, de, d), lambda j, te, nt: (te[j], 0, 0)),
                      pl.BlockSpec((None, 1, d), lambda j, te, nt: (te[j], 0, 0))],
            out_specs=pl.BlockSpec((MOE_TILE, d), lambda j, te, nt: (j, 0))),
        out_shape=jax.ShapeDtypeStruct((max_tiles * MOE_TILE, d), F32),
        compiler_params=_cparams(("arbitrary",)),
        name="moe_experts",
    )(tile_expert, n_tiles, hs, w_gu, b_gu.reshape(n_exp, 1, 2 * de), w_dn, b_dn.reshape(n_exp, 1, d))

    out = pl.pallas_call(
        _combine_kernel,
        grid=(nt_tok,),
        in_specs=[pl.BlockSpec((None, 1, TOK_TILE * TOP_K), lambda i: (i, 0, 0), memory_space=pltpu.SMEM),
                  pl.BlockSpec((TOK_TILE, d), lambda i: (i, 0)),
                  pl.BlockSpec((TOK_TILE, LANES), lambda i: (i, 0)),
                  pl.BlockSpec((None, 1, N_ADA * d), lambda i: (mod_map(i // n_bt, i % n_bt), 0, 0)),
                  pl.BlockSpec(memory_space=pl.ANY)],
        out_specs=pl.BlockSpec((TOK_TILE, d), lambda i: (i, 0)),
        out_shape=jax.ShapeDtypeStruct((n, d), F32),
        scratch_shapes=[pltpu.VMEM((TOP_K, TOK_TILE, d), F32), pltpu.SemaphoreType.DMA],
        input_output_aliases={1: 0},
        compiler_params=_cparams(("arbitrary",)),
        name="moe_combine",
    )(slot3, xs.reshape(n, d), route2, mods_l, ys)
    return out.reshape(b, s, d)


def kernel(x, c, ctx, c_ctx, ada_w, ada_b, norm1_g, norm2_g, gdn_w_in, gdn_conv_w, gdn_a_log, gdn_dt_bias, gdn_norm_g, gdn_w_out, gmlp_w_in, gmlp_b_in, gmlp_ln_g, gmlp_ln_b, gmlp_w_s, gmlp_b_s, gmlp_w_out, router_w, router_b, moe_w_gu, moe_b_gu, moe_w_dn, moe_b_dn, final_g):
    b, t, d = x.shape
    n_ctx = ctx.shape[1]
    s = n_ctx + t
    depth = ada_w.shape[0]
    n_exp = router_w.shape[-1]
    assert t % TOK_TILE == 0 and n_ctx % TOK_TILE == 0 and d % LANES == 0 and n_exp <= LANES
    assert (b * s * TOP_K) % MOE_TILE == 0
    n_bt = s // TOK_TILE
    n_ct = n_ctx // TOK_TILE
    qk_w = GDN_HEADS * GDN_DK
    qkv_w = 3 * qk_w

    n_rows = -(-(b + 1) // 8) * 8
    cc = jnp.zeros((n_rows, d), F32).at[:b].set(c).at[b].set(c_ctx)
    mods = _ada_mods(cc, ada_w, ada_b).reshape(depth, n_rows, 1, N_ADA * d)

    def mod_map(bi, ti):
        return jnp.where(ti < n_ct, b, bi)

    mod_spec = pl.BlockSpec((None, 1, N_ADA * d), lambda bi, ti: (mod_map(bi, ti), 0, 0))
    tok_spec = lambda w: pl.BlockSpec((None, TOK_TILE, w), lambda bi, ti: (bi, ti, 0))
    full = lambda shape: pl.BlockSpec(shape, lambda bi, ti: (0,) * len(shape))
    tok_shape = lambda w, dt: jax.ShapeDtypeStruct((b, s, w), dt)

    xs = jnp.concatenate([ctx, x], axis=1)

    route_out_specs = [tok_spec(d), tok_spec(d), tok_spec(LANES),
                       pl.BlockSpec((None, 1, LANES), lambda bi, ti: (bi * n_bt + ti, 0, 0))]
    route_out_shapes = [tok_shape(d, F32), tok_shape(d, F32), tok_shape(LANES, F32),
                        jax.ShapeDtypeStruct((b * n_bt, 1, LANES), F32)]

    for i in range(depth):
        j = i // 2
        mods_l = mods[i]
        g1 = norm1_g[i].reshape(1, d)
        g2 = norm2_g[i].reshape(1, d)
        wr = jnp.zeros((d, LANES), F32).at[:, :n_exp].set(router_w[i])
        wr_hi = wr.astype(BF16)
        wr_lo = (wr - wr_hi.astype(F32)).astype(BF16)
        br = jnp.full((1, LANES), -1e30, F32).at[0, :n_exp].set(router_b[i])
        if i % 2 == 0:
            w_in = gdn_w_in[j]
            w_main = w_in[:, :qkv_w + qk_w].astype(BF16)
            n_ab = w_in.shape[1] - (qkv_w + qk_w)
            w_ab = jnp.zeros((d, LANES), F32).at[:, :n_ab].set(w_in[:, qkv_w + qk_w:]).astype(BF16)
            ab_c = jnp.zeros((8, LANES), F32)
            ab_c = ab_c.at[0, :2 * GDN_HEADS].set(-jnp.exp(gdn_a_log[j].reshape(-1)))
            ab_c = ab_c.at[1, :2 * GDN_HEADS].set(gdn_dt_bias[j].reshape(-1))
            qkv, z, gb = pl.pallas_call(
                functools.partial(_gdn_in_kernel, qkv_w=qkv_w),
                grid=(b, n_bt),
                in_specs=[tok_spec(d), mod_spec, full((1, d)), full((d, qkv_w + qk_w)), full((d, LANES)),
                          full((8, LANES))],
                out_specs=[tok_spec(qkv_w), tok_spec(qk_w), tok_spec(LANES)],
                out_shape=[tok_shape(qkv_w, BF16), tok_shape(qk_w, BF16), tok_shape(LANES, F32)],
                compiler_params=_cparams(("parallel", "parallel")),
                name="gdn_in",
            )(xs, mods_l, g1, w_main, w_ab, ab_c)

            qkv_c = pl.pallas_call(
                functools.partial(_gdn_conv_kernel, n_ctx=n_ctx, n_lat=t, qk_blocks=2 * GDN_HEADS),
                grid=(b, qkv_w // LANES),
                in_specs=[pl.BlockSpec((None, s, LANES), lambda bi, ci: (bi, 0, ci)),
                          pl.BlockSpec((GDN_CONV, LANES), lambda bi, ci: (0, ci))],
                out_specs=pl.BlockSpec((None, s, LANES), lambda bi, ci: (bi, 0, ci)),
                out_shape=tok_shape(qkv_w, BF16),
                scratch_shapes=[pltpu.VMEM((s + 24, LANES), F32)],
                compiler_params=_cparams(("parallel", "parallel")),
                name="gdn_conv",
            )(qkv, gdn_conv_w[j])

            gbt = jnp.swapaxes(gb[:, :, :4 * GDN_HEADS], 1, 2)

            def bwd_tile(ti):
                return jnp.where(ti < n_ct, n_ct - 1 - ti, n_bt - 1 - (ti - n_ct))

            def hd_spec(off, tmap):
                return pl.BlockSpec((None, TOK_TILE, LANES), lambda bi, hi, ti: (bi, tmap(ti), off + hi))

            fwd_tile = lambda ti: ti
            scan_in = []
            for tmap in (fwd_tile, bwd_tile):
                scan_in += [hd_spec(0, tmap), hd_spec(GDN_HEADS, tmap), hd_spec(2 * GDN_HEADS, tmap),
                            pl.BlockSpec((None, TOK_TILE, LANES), lambda bi, hi, ti, tmap=tmap: (bi, tmap(ti), 0)),
                            pl.BlockSpec((None, 4 * GDN_HEADS, TOK_TILE),
                                         lambda bi, hi, ti, tmap=tmap: (bi, 0, tmap(ti)))]
            o_f, o_b = pl.pallas_call(
                functools.partial(_gdn_scan_kernel, scale=GDN_DK ** -0.5),
                grid=(b, GDN_HEADS, n_bt),
                in_specs=scan_in,
                out_specs=[hd_spec(0, fwd_tile), hd_spec(0, bwd_tile)],
                out_shape=[tok_shape(qk_w, BF16), tok_shape(qk_w, BF16)],
                scratch_shapes=[pltpu.VMEM((GDN_DK, GDN_DK), F32), pltpu.VMEM((GDN_DK, GDN_DK), F32)],
                compiler_params=_cparams(("parallel", "parallel", "arbitrary")),
                name="gdn_scan",
            )(qkv_c, qkv_c, qkv_c, gb, gbt, qkv_c, qkv_c, qkv_c, gb, gbt)

            xs, h2, route, counts = pl.pallas_call(
                _gdn_out_kernel,
                grid=(b, n_bt),
                in_specs=[tok_spec(qk_w), tok_spec(qk_w), tok_spec(qk_w), tok_spec(d), mod_spec,
                          full((1, LANES)), full((qk_w, d)), full((1, d)), full((d, LANES)), full((d, LANES)),
                          full((1, LANES))],
                out_specs=route_out_specs,
                out_shape=route_out_shapes,
                input_output_aliases={3: 0},
                compiler_params=_cparams(("parallel", "parallel")),
                name="gdn_out",
            )(o_f, o_b, z, xs, mods_l, gdn_norm_g[j].reshape(1, LANES), gdn_w_out[j].astype(BF16), g2,
              wr_hi, wr_lo, br)
        else:
            width = gmlp_w_out.shape[1]
            xs, h2, route, counts = pl.pallas_call(
                _gmlp_kernel,
                grid=(b, n_bt),
                in_specs=[tok_spec(d), mod_spec, full((1, d)), full((d, 2 * width)), full((1, 2 * width)),
                          full((1, width)), full((1, width)), full((GMLP_GROUPS, GMLP_CHUNK, GMLP_CHUNK)),
                          full((GMLP_CHUNK, GMLP_GROUPS)), full((width, d)), full((1, d)), full((d, LANES)),
                          full((d, LANES)), full((1, LANES))],
                out_specs=route_out_specs,
                out_shape=route_out_shapes,
                input_output_aliases={0: 0},
                compiler_params=_cparams(("parallel", "parallel")),
                name="gmlp",
            )(xs, mods_l, g1, gmlp_w_in[j].astype(BF16), gmlp_b_in[j].reshape(1, -1),
              gmlp_ln_g[j].reshape(1, -1), gmlp_ln_b[j].reshape(1, -1), gmlp_w_s[j].astype(BF16),
              gmlp_b_s[j].T, gmlp_w_out[j].astype(BF16), g2, wr_hi, wr_lo, br)

        xs = _moe(xs, h2, route, counts, mods_l, mod_map, moe_w_gu[i].astype(BF16), moe_b_gu[i],
                  moe_w_dn[i].astype(BF16), moe_b_dn[i], n_bt)

    return pl.pallas_call(
        _final_kernel,
        grid=(b, t // TOK_TILE),
        in_specs=[pl.BlockSpec((None, TOK_TILE, d), lambda bi, ti: (bi, n_ct + ti, 0)), full((1, d))],
        out_specs=pl.BlockSpec((None, TOK_TILE, d), lambda bi, ti: (bi, ti, 0)),
        out_shape=jax.ShapeDtypeStruct((b, t, d), x.dtype),
        compiler_params=_cparams(("parallel", "parallel")),
        name="final_norm",
    )(xs, final_g.reshape(1, d))
```

```python
import functools

import jax
import jax.numpy as jnp
from jax import lax
from jax.experimental import pallas as pl
from jax.experimental.pallas import tpu as pltpu

F32 = jnp.float32
BF16 = jnp.bfloat16

NORM_EPS = 1e-6
N_ADA = 6
GDN_HEADS = 8
GDN_DK = 128
GDN_CONV = 5
GDN_CHUNK = 64
GMLP_CHUNK = 128
GMLP_GROUPS = 8
TOP_K = 4
SWIGLU_ALPHA = 1.702
SWIGLU_LIMIT = 7.0

LANES = 128
TOK_TILE = 256
MOE_TILE = 512
INV_BLOCK = 16
SCAN_HEADS = 2
VMEM_LIMIT = 56 * 1024 * 1024


def _cparams(sem):
    return pltpu.CompilerParams(dimension_semantics=sem, vmem_limit_bytes=VMEM_LIMIT)


def _dot(a, b):
    return jnp.dot(a.astype(BF16), b.astype(BF16), preferred_element_type=F32)


def _dot_nt(a, b):
    return lax.dot_general(a.astype(BF16), b.astype(BF16), (((1,), (1,)), ((), ())),
                           preferred_element_type=F32)


def _dot_tn(a, b):
    return lax.dot_general(a.astype(BF16), b.astype(BF16), (((0,), (0,)), ((), ())),
                           preferred_element_type=F32)


def _split(a):
    hi = a.astype(BF16)
    lo = (a - hi.astype(F32)).astype(BF16)
    return hi, lo


def _dot3(a, b):
    ah, al = _split(a)
    bh, bl = _split(b)
    d = lambda x, y: jnp.dot(x, y, preferred_element_type=F32)
    return d(ah, bh) + (d(ah, bl) + d(al, bh))


def _rms_scale(xf):
    return xf * lax.rsqrt(jnp.mean(xf * xf, axis=-1, keepdims=True) + NORM_EPS)


def _silu(x):
    return x * jax.nn.sigmoid(x)


def _mod_part(m, k, d):
    return m[:, k * d:(k + 1) * d]


def _ada_kernel(cc_ref, w_ref, b_ref, o_ref):
    s = _silu(cc_ref[...])
    o_ref[...] = jnp.dot(s, w_ref[...], preferred_element_type=F32,
                         precision=lax.Precision.HIGHEST) + b_ref[...]


def _ada_mods(cc, ada_w, ada_b):
    depth, d, n = ada_w.shape
    r = cc.shape[0]
    tn = n // 4
    return pl.pallas_call(
        _ada_kernel,
        grid=(depth, n // tn),
        in_specs=[pl.BlockSpec((r, d), lambda l, j: (0, 0)),
                  pl.BlockSpec((None, d, tn), lambda l, j: (l, 0, j)),
                  pl.BlockSpec((None, 1, tn), lambda l, j: (l, 0, j))],
        out_specs=pl.BlockSpec((None, r, tn), lambda l, j: (l, 0, j)),
        out_shape=jax.ShapeDtypeStruct((depth, r, n), F32),
        compiler_params=_cparams(("parallel", "parallel")),
        name="ada_mods",
    )(cc, ada_w, ada_b.reshape(depth, 1, n))


def _route_epilogue(x_new, m, g2, wr_hi, wr_lo, br, tri, h2_ref, route_ref, cnt_ref):
    d = x_new.shape[-1]
    tm = x_new.shape[0]
    h2 = (_rms_scale(x_new) * g2) * (1.0 + _mod_part(m, 4, d)) + _mod_part(m, 3, d)
    h2_ref[...] = h2
    hh, hl = _split(h2)
    dd = lambda a, b: jnp.dot(a, b, preferred_element_type=F32)
    logits = dd(hh, wr_hi) + (dd(hh, wr_lo) + dd(hl, wr_hi)) + br
    lane = lax.broadcasted_iota(jnp.int32, (tm, LANES), 1).astype(F32)
    vals, idxs = [], []
    l = logits
    for _ in range(TOP_K):
        mx = jnp.max(l, axis=1, keepdims=True)
        ix = jnp.min(jnp.where(l == mx, lane, float(LANES)), axis=1, keepdims=True)
        vals.append(mx)
        idxs.append(ix)
        l = jnp.where(lane == ix, -jnp.inf, l)
    es = [jnp.exp(v - vals[0]) for v in vals]
    den = es[0] + es[1] + es[2] + es[3]
    hots = [(lane == ix).astype(F32) for ix in idxs]
    oh4 = hots[0] + hots[1] + hots[2] + hots[3]
    prefix = jnp.dot(tri, oh4.astype(BF16), preferred_element_type=F32)
    route = jnp.zeros((tm, LANES), F32)
    for k in range(TOP_K):
        rank = jnp.sum(hots[k] * prefix, axis=1, keepdims=True)
        route = jnp.where(lane == float(k), idxs[k], route)
        route = jnp.where(lane == float(TOP_K + k), es[k] / den, route)
        route = jnp.where(lane == float(2 * TOP_K + k), rank, route)
    route_ref[...] = route
    cnt_ref[...] = jnp.sum(oh4, axis=0, keepdims=True)


def _strict_lower(n):
    i = lax.broadcasted_iota(jnp.int32, (n, n), 0)
    j = lax.broadcasted_iota(jnp.int32, (n, n), 1)
    return (j < i).astype(BF16)


def _gdn_in_kernel(x_ref, m_ref, g1_ref, w_ref, wab_ref, ab_ref, qkv_ref, z_ref, gb_ref, *, qkv_w):
    d = x_ref.shape[-1]
    m = m_ref[...]
    h = (_rms_scale(x_ref[...]) * g1_ref[...]) * (1.0 + _mod_part(m, 1, d)) + _mod_part(m, 0, d)
    hb = h.astype(BF16)
    p = jnp.dot(hb, w_ref[...], preferred_element_type=F32)
    qkv_ref[...] = p[:, :qkv_w].astype(BF16)
    z_ref[...] = p[:, qkv_w:].astype(BF16)
    ab = jnp.dot(hb, wab_ref[...], preferred_element_type=F32)
    neg_a = ab_ref[0:1, :]
    dt_b = ab_ref[1:2, :]
    xs = ab + dt_b
    softplus = jnp.maximum(xs, 0.0) + jnp.log(1.0 + jnp.exp(-jnp.abs(xs)))
    lane = lax.broadcasted_iota(jnp.int32, ab.shape, 1)
    gb_ref[...] = jnp.where(lane < 2 * GDN_HEADS, neg_a * softplus, jax.nn.sigmoid(ab))


def _gdn_conv_kernel(x_ref, w_ref, o_ref, pad_ref, *, n_ctx, n_lat, qk_blocks):
    tile = TOK_TILE
    zeros8 = jnp.zeros((8, LANES), F32)
    pad_ref[0:8, :] = zeros8
    pad_ref[8 + n_ctx:16 + n_ctx, :] = zeros8
    pad_ref[16 + n_ctx + n_lat:24 + n_ctx + n_lat, :] = zeros8
    for r in range(0, n_ctx + n_lat, tile):
        base = 8 + r if r < n_ctx else 16 + r
        pad_ref[base:base + tile, :] = x_ref[r:r + tile, :].astype(F32)
    w = w_ref[...]
    is_qk = pl.program_id(1) < qk_blocks
    half = GDN_CONV // 2
    for r in range(0, n_ctx + n_lat, tile):
        base = 8 + r if r < n_ctx else 16 + r
        acc = jnp.zeros((tile, LANES), F32)
        for k in range(GDN_CONV):
            s = base + k - half
            acc = acc + pad_ref[s:s + tile, :] * w[k:k + 1, :]
        y = _silu(acc)
        yn = y * lax.rsqrt(jnp.sum(y * y, axis=-1, keepdims=True) + NORM_EPS)
        o_ref[r:r + tile, :] = jnp.where(is_qk, yn, y).astype(BF16)


def _each(fn, *seqs):
    return [fn(*args) for args in zip(*seqs)]


def _unit_tri_inverse(lmats, bi, bj):
    n = lmats[0].shape[0]
    eye = (lax.broadcasted_iota(jnp.int32, (n, n), 0) == lax.broadcasted_iota(jnp.int32, (n, n), 1)).astype(F32)
    ld = _each(lambda l: jnp.where(bi == bj, l, 0.0), lmats)
    x = _each(lambda l: eye - l, ld)
    p = _each(_dot, ld, ld)
    width = 2
    while True:
        x = _each(lambda xi, pi: xi + _dot(xi, pi), x, p)
        width *= 2
        if width >= INV_BLOCK:
            break
        p = _each(_dot, p, p)
    span = 2
    while INV_BLOCK * span <= GDN_CHUNK:
        half = span // 2
        off = ((bi // span) == (bj // span)) & ((bi // half) != (bj // half))
        xc = _each(lambda xi, l: _dot(xi, jnp.where(off, l, 0.0)), x, lmats)
        x = _each(lambda xi, xci: xi - _dot(xci, xi), x, xc)
        span *= 2
    return x


def _gdn_scan_kernel(qf_ref, kf_ref, vf_ref, gbf_ref, gtf_ref, qb_ref, kb_ref, vb_ref, gbb_ref, gtb_ref,
                     of_ref, ob_ref, sf_ref, sb_ref, *, scale):
    h0 = pl.program_id(1) * (qf_ref.shape[1] // GDN_DK)

    @pl.when(pl.program_id(2) == 0)
    def _():
        sf_ref[...] = jnp.zeros_like(sf_ref)
        sb_ref[...] = jnp.zeros_like(sb_ref)

    tm = qf_ref.shape[0]
    dk = GDN_DK
    hb = qf_ref.shape[1] // dk
    cs = GDN_CHUNK
    n_chunks = tm // cs
    lane = lax.broadcasted_iota(jnp.int32, (tm, LANES), 1)
    ii = lax.broadcasted_iota(jnp.int32, (tm, tm), 0)
    jj = lax.broadcasted_iota(jnp.int32, (tm, tm), 1)
    same = (ii // cs) == (jj // cs)
    bi, bj = ii // INV_BLOCK, jj // INV_BLOCK

    def col(gb, idx):
        return jnp.sum(jnp.where(lane == idx, gb, 0.0), axis=1, keepdims=True)

    probs = [(hh, d) for hh in range(hb) for d in range(2)]
    tri_incl = [same & (ii >= jj), same & (ii <= jj)]
    tri_strict = [same & (ii > jj), same & (ii < jj)]
    incl = [tri_incl[d] for _, d in probs]
    incl_t = [tri_incl[1 - d] for _, d in probs]
    strict = [tri_strict[d] for _, d in probs]
    gbs = [gbf_ref[...], gbb_ref[...]]
    gts = [gtf_ref, gtb_ref]
    qkv_refs = [(qf_ref, kf_ref, vf_ref), (qb_ref, kb_ref, vb_ref)]
    o_refs = [of_ref, ob_ref]
    s_refs = [sf_ref, sb_ref]
    g_col = [col(gbs[d], d * GDN_HEADS + h0 + hh) for hh, d in probs]
    beta = [col(gbs[d], (2 + d) * GDN_HEADS + h0 + hh) for hh, d in probs]
    g_row = [gts[d][pl.ds(d * GDN_HEADS + h0 + hh, 1), :] for hh, d in probs]
    q, k, v = [[qkv_refs[d][j][:, hh * dk:(hh + 1) * dk].astype(F32) for hh, d in probs] for j in range(3)]

    gc_col = _each(lambda m, g: jnp.sum(jnp.where(m, g, 0.0), axis=1, keepdims=True), incl, g_row)
    gc_row = _each(lambda m, g: jnp.sum(jnp.where(m, g, 0.0), axis=0, keepdims=True), incl_t, g_col)
    g_tot = _each(lambda g: jnp.sum(jnp.where(same, g, 0.0), axis=1, keepdims=True), g_row)
    decay = _each(lambda m, c, r: jnp.exp(jnp.where(m, c - r, -jnp.inf)), incl, gc_col, gc_row)
    qkk = _each(lambda qi, ki: _dot_nt(jnp.concatenate([qi, ki], axis=0), ki), q, k)
    lmat = _each(lambda m, b, x, dc: jnp.where(m, b * x[tm:] * dc, 0.0), strict, beta, qkk, decay)
    t_inv = _unit_tri_inverse(lmat, bi, bj)
    eg = _each(jnp.exp, gc_col)
    wu = _each(lambda t, ki, vi, b, e: _dot(t, jnp.concatenate([ki * (b * e), vi * b], axis=1)),
               t_inv, k, v, beta, eg)
    awu = _each(lambda x, dc, w: _dot(x[:tm] * (scale * dc), w), qkk, decay, wu)
    q_eff = _each(lambda qi, e, a: qi * (scale * e) - a[:, :dk], q, eg, awu)
    k_tail = _each(lambda ki, t, c: ki * jnp.exp(t - c), k, g_tot, gc_col)
    s = [s_refs[d][hh] for hh, d in probs]
    for step in range(n_chunks):
        for p, (hh, d) in enumerate(probs):
            c = step if d == 0 else n_chunks - 1 - step
            sl = slice(c * cs, (c + 1) * cs)
            kwu = _dot_tn(k_tail[p][sl], wu[p][sl])
            o_refs[d][sl, hh * dk:(hh + 1) * dk] = (
                _dot(q_eff[p][sl], s[p]) + awu[p][sl, dk:]).astype(o_refs[d].dtype)
            s[p] = s[p] * jnp.exp(g_tot[p][c * cs:c * cs + 1, :]) + (kwu[:, dk:] - _dot(kwu[:, :dk], s[p]))
    for p, (hh, d) in enumerate(probs):
        s_refs[d][hh] = s[p]


def _gdn_out_kernel(of_ref, ob_ref, z_ref, x_ref, m_ref, ng_ref, w_ref, g2_ref, wrh_ref, wrl_ref, br_ref,
                    xo_ref, h2_ref, route_ref, cnt_ref):
    d = x_ref.shape[-1]
    m = m_ref[...]
    o = of_ref[...].astype(F32) + ob_ref[...].astype(F32)
    z = z_ref[...].astype(F32)
    ng = ng_ref[...]
    parts = []
    for hd in range(o.shape[1] // LANES):
        sl = slice(hd * LANES, (hd + 1) * LANES)
        parts.append(_rms_scale(o[:, sl]) * ng * _silu(z[:, sl]))
    y = jnp.dot(jnp.concatenate(parts, axis=1).astype(BF16), w_ref[...], preferred_element_type=F32)
    x_new = x_ref[...] + _mod_part(m, 2, d) * y
    xo_ref[...] = x_new
    _route_epilogue(x_new, m, g2_ref[...], wrh_ref[...], wrl_ref[...], br_ref[...],
                    _strict_lower(x_new.shape[0]), h2_ref, route_ref, cnt_ref)


def _gmlp_kernel(x_ref, m_ref, g1_ref, win_ref, bin_ref, lng_ref, lnb_ref, ws_ref, bs_ref, wout_ref,
                 g2_ref, wrh_ref, wrl_ref, br_ref, xo_ref, h2_ref, route_ref, cnt_ref):
    d = x_ref.shape[-1]
    tm = x_ref.shape[0]
    m = m_ref[...]
    x = x_ref[...]
    h = (_rms_scale(x) * g1_ref[...]) * (1.0 + _mod_part(m, 1, d)) + _mod_part(m, 0, d)
    zz = jnp.dot(h.astype(BF16), win_ref[...], preferred_element_type=F32) + bin_ref[...]
    zz = 0.5 * zz * (1.0 + lax.erf(zz * (2.0 ** -0.5)))
    width = zz.shape[1] // 2
    u, v = zz[:, :width], zz[:, width:]
    mu = jnp.mean(v, axis=-1, keepdims=True)
    vc = v - mu
    v = (vc * lax.rsqrt(jnp.mean(vc * vc, axis=-1, keepdims=True) + NORM_EPS)) * lng_ref[...] + lnb_ref[...]
    vb = v.astype(BF16)
    gw = width // GMLP_GROUPS
    rows = []
    for c in range(tm // GMLP_CHUNK):
        rs = slice(c * GMLP_CHUNK, (c + 1) * GMLP_CHUNK)
        cols = []
        for g in range(GMLP_GROUPS):
            sg = jnp.dot(ws_ref[g], vb[rs, g * gw:(g + 1) * gw], preferred_element_type=F32)
            cols.append(sg + bs_ref[:, g:g + 1])
        rows.append(jnp.concatenate(cols, axis=1))
    s = jnp.concatenate(rows, axis=0)
    y = jnp.dot((u * s).astype(BF16), wout_ref[...], preferred_element_type=F32)
    x_new = x + _mod_part(m, 2, d) * y
    xo_ref[...] = x_new
    _route_epilogue(x_new, m, g2_ref[...], wrh_ref[...], wrl_ref[...], br_ref[...],
                    _strict_lower(tm), h2_ref, route_ref, cnt_ref)


def _dispatch_kernel(slot_ref, h_ref, init_ref, o_ref, sem):
    del init_ref
    n = h_ref.shape[0]

    def row_copy(r, k):
        return pltpu.make_async_copy(h_ref.at[pl.ds(r, 1), :],
                                     o_ref.at[pl.ds(slot_ref[0, r * TOP_K + k], 1), :], sem)

    def start(r, c):
        for k in range(TOP_K):
            row_copy(r, k).start(priority=k % 2)
        return c

    def wait(r, c):
        for k in range(TOP_K):
            row_copy(r, k).wait()
        return c

    lax.fori_loop(0, n, start, 0)
    lax.fori_loop(0, n, wait, 0)


def _expert_kernel(te_ref, nt_ref, x_ref, wgu_ref, bgu_ref, wdn_ref, bdn_ref, o_ref):
    del te_ref

    @pl.when(pl.program_id(0) < nt_ref[0])
    def _():
        de = wdn_ref.shape[0]
        gu = jnp.dot(x_ref[...].astype(BF16), wgu_ref[...], preferred_element_type=F32) + bgu_ref[...]
        gate = jnp.minimum(gu[:, :de], SWIGLU_LIMIT)
        up = jnp.clip(gu[:, de:], -SWIGLU_LIMIT, SWIGLU_LIMIT)
        glu = gate * jax.nn.sigmoid(gate * SWIGLU_ALPHA)
        act = ((up + 1.0) * glu).astype(BF16)
        o_ref[...] = jnp.dot(act, wdn_ref[...], preferred_element_type=F32) + bdn_ref[...]

    @pl.when(pl.program_id(0) >= nt_ref[0])
    def _():
        o_ref[...] = jnp.zeros_like(o_ref)


def _combine_kernel(slot_ref, x_ref, route_ref, m_ref, y_ref, o_ref, buf_ref, sem):
    n, d = x_ref.shape

    def row_copy(r, k):
        return pltpu.make_async_copy(y_ref.at[pl.ds(slot_ref[0, r * TOP_K + k], 1), :],
                                     buf_ref.at[k, pl.ds(r, 1), :], sem)

    def start(r, c):
        for k in range(TOP_K):
            row_copy(r, k).start(priority=k % 2)
        return c

    def wait(r, c):
        for k in range(TOP_K):
            row_copy(r, k).wait()
        return c

    lax.fori_loop(0, n, start, 0)
    lax.fori_loop(0, n, wait, 0)
    route = route_ref[...]
    acc = jnp.zeros((n, d), F32)
    for k in range(TOP_K):
        acc = acc + route[:, TOP_K + k:TOP_K + k + 1] * buf_ref[k]
    o_ref[...] = x_ref[...] + _mod_part(m_ref[...], 5, d) * acc


def _final_kernel(x_ref, g_ref, o_ref):
    o_ref[...] = _rms_scale(x_ref[...]) * g_ref[...]


def _moe(xs, h2, route, counts, mods_l, mod_map, w_gu, b_gu, w_dn, b_dn, n_bt):
    b, s, d = xs.shape
    n = b * s
    n_exp = w_gu.shape[0]
    nt_tok = n // TOK_TILE
    max_tiles = (n * TOP_K) // MOE_TILE + n_exp
    route2 = route.reshape(n, LANES)
    top_i = route2[:, :TOP_K].astype(jnp.int32)
    rank = route2[:, 2 * TOP_K:3 * TOP_K].astype(jnp.int32)
    cnt = counts.reshape(nt_tok, LANES)[:, :n_exp].astype(jnp.int32)
    totals = jnp.sum(cnt, axis=0)
    tiles_e = (totals + MOE_TILE - 1) // MOE_TILE
    tile_end = jnp.cumsum(tiles_e)
    starts = (tile_end - tiles_e) * MOE_TILE
    base = starts[None, :] + jnp.cumsum(cnt, axis=0) - cnt
    hit = top_i.reshape(nt_tok, TOK_TILE, TOP_K, 1) == jnp.arange(n_exp, dtype=jnp.int32)
    slot = jnp.sum(jnp.where(hit, base[:, None, None, :], 0), axis=-1).reshape(n, TOP_K) + rank
    slot3 = slot.reshape(nt_tok, 1, TOK_TILE * TOP_K)
    tile_ids = jnp.arange(max_tiles, dtype=jnp.int32)
    tile_expert = jnp.minimum(jnp.sum((tile_end[None, :] <= tile_ids[:, None]).astype(jnp.int32), axis=1),
                              n_exp - 1)
    n_tiles = tile_end[-1:].astype(jnp.int32)

    h2f = h2.reshape(n, d)
    hs = pl.pallas_call(
        _dispatch_kernel,
        grid=(nt_tok,),
        in_specs=[pl.BlockSpec((None, 1, TOK_TILE * TOP_K), lambda i: (i, 0, 0), memory_space=pltpu.SMEM),
                  pl.BlockSpec((TOK_TILE, d), lambda i: (i, 0)),
                  pl.BlockSpec(memory_space=pl.ANY)],
        out_specs=pl.BlockSpec(memory_space=pl.ANY),
        out_shape=jax.ShapeDtypeStruct((max_tiles * MOE_TILE, d), F32),
        scratch_shapes=[pltpu.SemaphoreType.DMA],
        input_output_aliases={2: 0},
        compiler_params=_cparams(("arbitrary",)),
        name="moe_dispatch",
    )(slot3, h2f, jnp.zeros((max_tiles * MOE_TILE, d), F32))

    de = w_dn.shape[1]
    ys = pl.pallas_call(
        _expert_kernel,
        grid_spec=pltpu.PrefetchScalarGridSpec(
            num_scalar_prefetch=2,
            grid=(max_tiles,),
            in_specs=[pl.BlockSpec((MOE_TILE, d), lambda j, te, nt: (j, 0)),
                      pl.BlockSpec((None, d, 2 * de), lambda j, te, nt: (te[j], 0, 0)),
                      pl.BlockSpec((None, 1, 2 * de), lambda j, te, nt: (te[j], 0, 0)),
                      pl.BlockSpec((None, de, d), lambda j, te, nt: (te[j], 0, 0)),
                      pl.BlockSpec((None, 1, d), lambda j, te, nt: (te[j], 0, 0))],
            out_specs=pl.BlockSpec((MOE_TILE, d), lambda j, te, nt: (j, 0))),
        out_shape=jax.ShapeDtypeStruct((max_tiles * MOE_TILE, d), F32),
        compiler_params=_cparams(("arbitrary",)),
        name="moe_experts",
    )(tile_expert, n_tiles, hs, w_gu, b_gu.reshape(n_exp, 1, 2 * de), w_dn, b_dn.reshape(n_exp, 1, d))

    out = pl.pallas_call(
        _combine_kernel,
        grid=(nt_tok,),
        in_specs=[pl.BlockSpec((None, 1, TOK_TILE * TOP_K), lambda i: (i, 0, 0), memory_space=pltpu.SMEM),
                  pl.BlockSpec((TOK_TILE, d), lambda i: (i, 0)),
                  pl.BlockSpec((TOK_TILE, LANES), lambda i: (i, 0)),
                  pl.BlockSpec((None, 1, N_ADA * d), lambda i: (mod_map(i // n_bt, i % n_bt), 0, 0)),
                  pl.BlockSpec(memory_space=pl.ANY)],
        out_specs=pl.BlockSpec((TOK_TILE, d), lambda i: (i, 0)),
        out_shape=jax.ShapeDtypeStruct((n, d), F32),
        scratch_shapes=[pltpu.VMEM((TOP_K, TOK_TILE, d), F32), pltpu.SemaphoreType.DMA],
        input_output_aliases={1: 0},
        compiler_params=_cparams(("arbitrary",)),
        name="moe_combine",
    )(slot3, xs.reshape(n, d), route2, mods_l, ys)
    return out.reshape(b, s, d)


def kernel(x, c, ctx, c_ctx, ada_w, ada_b, norm1_g, norm2_g, gdn_w_in, gdn_conv_w, gdn_a_log, gdn_dt_bias, gdn_norm_g, gdn_w_out, gmlp_w_in, gmlp_b_in, gmlp_ln_g, gmlp_ln_b, gmlp_w_s, gmlp_b_s, gmlp_w_out, router_w, router_b, moe_w_gu, moe_b_gu, moe_w_dn, moe_b_dn, final_g):
    b, t, d = x.shape
    n_ctx = ctx.shape[1]
    s = n_ctx + t
    depth = ada_w.shape[0]
    n_exp = router_w.shape[-1]
    assert t % TOK_TILE == 0 and n_ctx % TOK_TILE == 0 and d % LANES == 0 and n_exp <= LANES
    assert (b * s * TOP_K) % MOE_TILE == 0
    n_bt = s // TOK_TILE
    n_ct = n_ctx // TOK_TILE
    qk_w = GDN_HEADS * GDN_DK
    qkv_w = 3 * qk_w

    n_rows = -(-(b + 1) // 8) * 8
    cc = jnp.zeros((n_rows, d), F32).at[:b].set(c).at[b].set(c_ctx)
    mods = _ada_mods(cc, ada_w, ada_b).reshape(depth, n_rows, 1, N_ADA * d)

    def mod_map(bi, ti):
        return jnp.where(ti < n_ct, b, bi)

    mod_spec = pl.BlockSpec((None, 1, N_ADA * d), lambda bi, ti: (mod_map(bi, ti), 0, 0))
    tok_spec = lambda w: pl.BlockSpec((None, TOK_TILE, w), lambda bi, ti: (bi, ti, 0))
    full = lambda shape: pl.BlockSpec(shape, lambda bi, ti: (0,) * len(shape))
    tok_shape = lambda w, dt: jax.ShapeDtypeStruct((b, s, w), dt)

    xs = jnp.concatenate([ctx, x], axis=1)

    route_out_specs = [tok_spec(d), tok_spec(d), tok_spec(LANES),
                       pl.BlockSpec((None, 1, LANES), lambda bi, ti: (bi * n_bt + ti, 0, 0))]
    route_out_shapes = [tok_shape(d, F32), tok_shape(d, F32), tok_shape(LANES, F32),
                        jax.ShapeDtypeStruct((b * n_bt, 1, LANES), F32)]

    for i in range(depth):
        j = i // 2
        mods_l = mods[i]
        g1 = norm1_g[i].reshape(1, d)
        g2 = norm2_g[i].reshape(1, d)
        wr = jnp.zeros((d, LANES), F32).at[:, :n_exp].set(router_w[i])
        wr_hi = wr.astype(BF16)
        wr_lo = (wr - wr_hi.astype(F32)).astype(BF16)
        br = jnp.full((1, LANES), -1e30, F32).at[0, :n_exp].set(router_b[i])
        if i % 2 == 0:
            w_in = gdn_w_in[j]
            w_main = w_in[:, :qkv_w + qk_w].astype(BF16)
            n_ab = w_in.shape[1] - (qkv_w + qk_w)
            w_ab = jnp.zeros((d, LANES), F32).at[:, :n_ab].set(w_in[:, qkv_w + qk_w:]).astype(BF16)
            ab_c = jnp.zeros((8, LANES), F32)
            ab_c = ab_c.at[0, :2 * GDN_HEADS].set(-jnp.exp(gdn_a_log[j].reshape(-1)))
            ab_c = ab_c.at[1, :2 * GDN_HEADS].set(gdn_dt_bias[j].reshape(-1))
            qkv, z, gb = pl.pallas_call(
                functools.partial(_gdn_in_kernel, qkv_w=qkv_w),
                grid=(b, n_bt),
                in_specs=[tok_spec(d), mod_spec, full((1, d)), full((d, qkv_w + qk_w)), full((d, LANES)),
                          full((8, LANES))],
                out_specs=[tok_spec(qkv_w), tok_spec(qk_w), tok_spec(LANES)],
                out_shape=[tok_shape(qkv_w, BF16), tok_shape(qk_w, BF16), tok_shape(LANES, F32)],
                compiler_params=_cparams(("parallel", "parallel")),
                name="gdn_in",
            )(xs, mods_l, g1, w_main, w_ab, ab_c)

            qkv_c = pl.pallas_call(
                functools.partial(_gdn_conv_kernel, n_ctx=n_ctx, n_lat=t, qk_blocks=2 * GDN_HEADS),
                grid=(b, qkv_w // LANES),
                in_specs=[pl.BlockSpec((None, s, LANES), lambda bi, ci: (bi, 0, ci)),
                          pl.BlockSpec((GDN_CONV, LANES), lambda bi, ci: (0, ci))],
                out_specs=pl.BlockSpec((None, s, LANES), lambda bi, ci: (bi, 0, ci)),
                out_shape=tok_shape(qkv_w, BF16),
                scratch_shapes=[pltpu.VMEM((s + 24, LANES), F32)],
                compiler_params=_cparams(("parallel", "parallel")),
                name="gdn_conv",
            )(qkv, gdn_conv_w[j])

            gbt = jnp.swapaxes(gb[:, :, :4 * GDN_HEADS], 1, 2)

            def bwd_tile(ti):
                return jnp.where(ti < n_ct, n_ct - 1 - ti, n_bt - 1 - (ti - n_ct))

            hw = SCAN_HEADS * GDN_DK
            n_hb = GDN_HEADS // SCAN_HEADS

            def hd_spec(off, tmap):
                return pl.BlockSpec((None, TOK_TILE, hw), lambda bi, hi, ti: (bi, tmap(ti), off + hi))

            fwd_tile = lambda ti: ti
            scan_in = []
            for tmap in (fwd_tile, bwd_tile):
                scan_in += [hd_spec(0, tmap), hd_spec(n_hb, tmap), hd_spec(2 * n_hb, tmap),
                            pl.BlockSpec((None, TOK_TILE, LANES), lambda bi, hi, ti, tmap=tmap: (bi, tmap(ti), 0)),
                            pl.BlockSpec((None, 4 * GDN_HEADS, TOK_TILE),
                                         lambda bi, hi, ti, tmap=tmap: (bi, 0, tmap(ti)))]
            o_f, o_b = pl.pallas_call(
                functools.partial(_gdn_scan_kernel, scale=GDN_DK ** -0.5),
                grid=(b, n_hb, n_bt),
                in_specs=scan_in,
                out_specs=[hd_spec(0, fwd_tile), hd_spec(0, bwd_tile)],
                out_shape=[tok_shape(qk_w, BF16), tok_shape(qk_w, BF16)],
                scratch_shapes=[pltpu.VMEM((SCAN_HEADS, GDN_DK, GDN_DK), F32),
                                pltpu.VMEM((SCAN_HEADS, GDN_DK, GDN_DK), F32)],
                compiler_params=_cparams(("parallel", "parallel", "arbitrary")),
                name="gdn_scan",
            )(qkv_c, qkv_c, qkv_c, gb, gbt, qkv_c, qkv_c, qkv_c, gb, gbt)

            xs, h2, route, counts = pl.pallas_call(
                _gdn_out_kernel,
                grid=(b, n_bt),
                in_specs=[tok_spec(qk_w), tok_spec(qk_w), tok_spec(qk_w), tok_spec(d), mod_spec,
                          full((1, LANES)), full((qk_w, d)), full((1, d)), full((d, LANES)), full((d, LANES)),
                          full((1, LANES))],
                out_specs=route_out_specs,
                out_shape=route_out_shapes,
                input_output_aliases={3: 0},
                compiler_params=_cparams(("parallel", "parallel")),
                name="gdn_out",
            )(o_f, o_b, z, xs, mods_l, gdn_norm_g[j].reshape(1, LANES), gdn_w_out[j].astype(BF16), g2,
              wr_hi, wr_lo, br)
        else:
            width = gmlp_w_out.shape[1]
            xs, h2, route, counts = pl.pallas_call(
                _gmlp_kernel,
                grid=(b, n_bt),
                in_specs=[tok_spec(d), mod_spec, full((1, d)), full((d, 2 * width)), full((1, 2 * width)),
                          full((1, width)), full((1, width)), full((GMLP_GROUPS, GMLP_CHUNK, GMLP_CHUNK)),
                          full((GMLP_CHUNK, GMLP_GROUPS)), full((width, d)), full((1, d)), full((d, LANES)),
                          full((d, LANES)), full((1, LANES))],
                out_specs=route_out_specs,
                out_shape=route_out_shapes,
                input_output_aliases={0: 0},
                compiler_params=_cparams(("parallel", "parallel")),
                name="gmlp",
            )(xs, mods_l, g1, gmlp_w_in[j].astype(BF16), gmlp_b_in[j].reshape(1, -1),
              gmlp_ln_g[j].reshape(1, -1), gmlp_ln_b[j].reshape(1, -1), gmlp_w_s[j].astype(BF16),
              gmlp_b_s[j].T, gmlp_w_out[j].astype(BF16), g2, wr_hi, wr_lo, br)

        xs = _moe(xs, h2, route, counts, mods_l, mod_map, moe_w_gu[i].astype(BF16), moe_b_gu[i],
                  moe_w_dn[i].astype(BF16), moe_b_dn[i], n_bt)

    return pl.pallas_call(
        _final_kernel,
        grid=(b, t // TOK_TILE),
        in_specs=[pl.BlockSpec((None, TOK_TILE, d), lambda bi, ti: (bi, n_ct + ti, 0)), full((1, d))],
        out_specs=pl.BlockSpec((None, TOK_TILE, d), lambda bi, ti: (bi, ti, 0)),
        out_shape=jax.ShapeDtypeStruct((b, t, d), x.dtype),
        compiler_params=_cparams(("parallel", "parallel")),
        name="final_norm",
    )(xs, final_g.reshape(1, d))
```

```python
import functools

import jax
import jax.numpy as jnp
from jax import lax
from jax.experimental import pallas as pl
from jax.experimental.pallas import tpu as pltpu

F32 = jnp.float32
BF16 = jnp.bfloat16

NORM_EPS = 1e-6
N_ADA = 6
GDN_HEADS = 8
GDN_DK = 128
GDN_CONV = 5
GDN_CHUNK = 64
GMLP_CHUNK = 128
GMLP_GROUPS = 8
TOP_K = 4
SWIGLU_ALPHA = 1.702
SWIGLU_LIMIT = 7.0

LANES = 128
SUBLANES = 8
TOK_TILE = 256
MOE_TILE = 512
INV_BLOCK = 16
SCAN_HEADS = 2
VMEM_LIMIT = 56 * 1024 * 1024


def _cparams(sem):
    return pltpu.CompilerParams(dimension_semantics=sem, vmem_limit_bytes=VMEM_LIMIT)


def _dot(a, b):
    return jnp.dot(a.astype(BF16), b.astype(BF16), preferred_element_type=F32)


def _dot_nt(a, b):
    return lax.dot_general(a.astype(BF16), b.astype(BF16), (((1,), (1,)), ((), ())),
                           preferred_element_type=F32)


def _dot_tn(a, b):
    return lax.dot_general(a.astype(BF16), b.astype(BF16), (((0,), (0,)), ((), ())),
                           preferred_element_type=F32)


def _split(a):
    hi = a.astype(BF16)
    lo = (a - hi.astype(F32)).astype(BF16)
    return hi, lo


def _dot3(a, b):
    ah, al = _split(a)
    bh, bl = _split(b)
    d = lambda x, y: jnp.dot(x, y, preferred_element_type=F32)
    return d(ah, bh) + (d(ah, bl) + d(al, bh))


def _rms_scale(xf):
    return xf * lax.rsqrt(jnp.mean(xf * xf, axis=-1, keepdims=True) + NORM_EPS)


def _silu(x):
    return x * jax.nn.sigmoid(x)


def _mod_part(m, k, d):
    return m[:, k * d:(k + 1) * d]


def _ada_kernel(cc_ref, w_ref, b_ref, o_ref):
    s = _silu(cc_ref[...])
    o_ref[...] = _dot3(s, w_ref[...]) + b_ref[...]


def _ada_mods(cc, ada_w, ada_b):
    depth, d, n = ada_w.shape
    r = cc.shape[0]
    tn = n // 4
    return pl.pallas_call(
        _ada_kernel,
        grid=(depth, n // tn),
        in_specs=[pl.BlockSpec((r, d), lambda l, j: (0, 0)),
                  pl.BlockSpec((None, d, tn), lambda l, j: (l, 0, j)),
                  pl.BlockSpec((None, 1, tn), lambda l, j: (l, 0, j))],
        out_specs=pl.BlockSpec((None, r, tn), lambda l, j: (l, 0, j)),
        out_shape=jax.ShapeDtypeStruct((depth, r, n), F32),
        compiler_params=_cparams(("parallel", "parallel")),
        name="ada_mods",
    )(cc, ada_w, ada_b.reshape(depth, 1, n))


def _route_epilogue(x_new, m, g2, wr_hi, wr_lo, br, tri, h2_ref, route_ref, cnt_ref):
    d = x_new.shape[-1]
    tm = x_new.shape[0]
    h2 = (_rms_scale(x_new) * g2) * (1.0 + _mod_part(m, 4, d)) + _mod_part(m, 3, d)
    h2_ref[...] = h2
    hh, hl = _split(h2)
    dd = lambda a, b: jnp.dot(a, b, preferred_element_type=F32)
    logits = dd(hh, wr_hi) + (dd(hh, wr_lo) + dd(hl, wr_hi)) + br
    lane = lax.broadcasted_iota(jnp.int32, (tm, LANES), 1).astype(F32)
    vals, idxs = [], []
    l = logits
    for _ in range(TOP_K):
        mx = jnp.max(l, axis=1, keepdims=True)
        ix = jnp.min(jnp.where(l == mx, lane, float(LANES)), axis=1, keepdims=True)
        vals.append(mx)
        idxs.append(ix)
        l = jnp.where(lane == ix, -jnp.inf, l)
    es = [jnp.exp(v - vals[0]) for v in vals]
    den = es[0] + es[1] + es[2] + es[3]
    hots = [(lane == ix).astype(F32) for ix in idxs]
    oh4 = hots[0] + hots[1] + hots[2] + hots[3]
    prefix = jnp.dot(tri, oh4.astype(BF16), preferred_element_type=F32)
    route = jnp.zeros((tm, LANES), F32)
    for k in range(TOP_K):
        rank = jnp.sum(hots[k] * prefix, axis=1, keepdims=True)
        route = jnp.where(lane == float(k), idxs[k], route)
        route = jnp.where(lane == float(TOP_K + k), es[k] / den, route)
        route = jnp.where(lane == float(2 * TOP_K + k), rank, route)
    route_ref[...] = route
    cnt_ref[...] = jnp.sum(oh4, axis=0, keepdims=True)


def _strict_lower(n):
    i = lax.broadcasted_iota(jnp.int32, (n, n), 0)
    j = lax.broadcasted_iota(jnp.int32, (n, n), 1)
    return (j < i).astype(BF16)


def _gdn_in_kernel(x_ref, m_ref, g1_ref, w_ref, wab_ref, ab_ref, qkv_ref, z_ref, gb_ref, *, qkv_w):
    d = x_ref.shape[-1]
    m = m_ref[...]
    h = (_rms_scale(x_ref[...]) * g1_ref[...]) * (1.0 + _mod_part(m, 1, d)) + _mod_part(m, 0, d)
    hb = h.astype(BF16)
    p = jnp.dot(hb, w_ref[...], preferred_element_type=F32)
    qkv_ref[...] = p[:, :qkv_w].astype(BF16)
    z_ref[...] = p[:, qkv_w:].astype(BF16)
    ab = jnp.dot(hb, wab_ref[...], preferred_element_type=F32)
    neg_a = ab_ref[0:1, :]
    dt_b = ab_ref[1:2, :]
    xs = ab + dt_b
    softplus = jnp.maximum(xs, 0.0) + jnp.log(1.0 + jnp.exp(-jnp.abs(xs)))
    lane = lax.broadcasted_iota(jnp.int32, ab.shape, 1)
    gb_ref[...] = jnp.where(lane < 2 * GDN_HEADS, neg_a * softplus, jax.nn.sigmoid(ab))


def _gdn_conv_kernel(x_ref, w_ref, o_ref, pad_ref, *, n_ctx, n_lat, qk_blocks):
    tile = TOK_TILE
    zeros8 = jnp.zeros((8, LANES), F32)
    pad_ref[0:8, :] = zeros8
    pad_ref[8 + n_ctx:16 + n_ctx, :] = zeros8
    pad_ref[16 + n_ctx + n_lat:24 + n_ctx + n_lat, :] = zeros8
    for r in range(0, n_ctx + n_lat, tile):
        base = 8 + r if r < n_ctx else 16 + r
        pad_ref[base:base + tile, :] = x_ref[r:r + tile, :].astype(F32)
    w = w_ref[...]
    is_qk = pl.program_id(1) < qk_blocks
    half = GDN_CONV // 2
    for r in range(0, n_ctx + n_lat, tile):
        base = 8 + r if r < n_ctx else 16 + r
        acc = jnp.zeros((tile, LANES), F32)
        for k in range(GDN_CONV):
            s = base + k - half
            acc = acc + pad_ref[s:s + tile, :] * w[k:k + 1, :]
        y = _silu(acc)
        yn = y * lax.rsqrt(jnp.sum(y * y, axis=-1, keepdims=True) + NORM_EPS)
        o_ref[r:r + tile, :] = jnp.where(is_qk, yn, y).astype(BF16)


def _each(fn, *seqs):
    return [fn(*args) for args in zip(*seqs)]


def _unit_tri_inverse(lmats, bi, bj):
    n = lmats[0].shape[0]
    eye = (lax.broadcasted_iota(jnp.int32, (n, n), 0) == lax.broadcasted_iota(jnp.int32, (n, n), 1)).astype(F32)
    ld = _each(lambda l: jnp.where(bi == bj, l, 0.0), lmats)
    x = _each(lambda l: eye - l, ld)
    p = _each(_dot, ld, ld)
    width = 2
    while True:
        x = _each(lambda xi, pi: xi + _dot(xi, pi), x, p)
        width *= 2
        if width >= INV_BLOCK:
            break
        p = _each(_dot, p, p)
    span = 2
    while INV_BLOCK * span <= GDN_CHUNK:
        half = span // 2
        off = ((bi // span) == (bj // span)) & ((bi // half) != (bj // half))
        xc = _each(lambda xi, l: _dot(xi, jnp.where(off, l, 0.0)), x, lmats)
        x = _each(lambda xi, xci: xi - _dot(xci, xi), x, xc)
        span *= 2
    return x


def _gdn_scan_kernel(qf_ref, kf_ref, vf_ref, gbf_ref, gtf_ref, qb_ref, kb_ref, vb_ref, gbb_ref, gtb_ref,
                     of_ref, ob_ref, sf_ref, sb_ref, *, scale):
    h0 = pl.program_id(1) * (qf_ref.shape[1] // GDN_DK)

    @pl.when(pl.program_id(2) == 0)
    def _():
        sf_ref[...] = jnp.zeros_like(sf_ref)
        sb_ref[...] = jnp.zeros_like(sb_ref)

    tm = qf_ref.shape[0]
    dk = GDN_DK
    hb = qf_ref.shape[1] // dk
    cs = GDN_CHUNK
    n_chunks = tm // cs
    lane = lax.broadcasted_iota(jnp.int32, (tm, LANES), 1)
    ii = lax.broadcasted_iota(jnp.int32, (tm, tm), 0)
    jj = lax.broadcasted_iota(jnp.int32, (tm, tm), 1)
    same = (ii // cs) == (jj // cs)
    bi, bj = ii // INV_BLOCK, jj // INV_BLOCK

    def col(gb, idx):
        return jnp.sum(jnp.where(lane == idx, gb, 0.0), axis=1, keepdims=True)

    probs = [(hh, d) for hh in range(hb) for d in range(2)]
    tri_incl = [same & (ii >= jj), same & (ii <= jj)]
    tri_strict = [same & (ii > jj), same & (ii < jj)]
    incl = [tri_incl[d] for _, d in probs]
    incl_t = [tri_incl[1 - d] for _, d in probs]
    strict = [tri_strict[d] for _, d in probs]
    gbs = [gbf_ref[...], gbb_ref[...]]
    gts = [gtf_ref, gtb_ref]
    qkv_refs = [(qf_ref, kf_ref, vf_ref), (qb_ref, kb_ref, vb_ref)]
    o_refs = [of_ref, ob_ref]
    s_refs = [sf_ref, sb_ref]
    g_col = [col(gbs[d], d * GDN_HEADS + h0 + hh) for hh, d in probs]
    beta = [col(gbs[d], (2 + d) * GDN_HEADS + h0 + hh) for hh, d in probs]
    g_row = [gts[d][pl.ds(d * GDN_HEADS + h0 + hh, 1), :] for hh, d in probs]
    q, k, v = [[qkv_refs[d][j][:, hh * dk:(hh + 1) * dk].astype(F32) for hh, d in probs] for j in range(3)]

    gc_col = _each(lambda m, g: jnp.sum(jnp.where(m, g, 0.0), axis=1, keepdims=True), incl, g_row)
    gc_row = _each(lambda m, g: jnp.sum(jnp.where(m, g, 0.0), axis=0, keepdims=True), incl_t, g_col)
    g_tot = _each(lambda g: jnp.sum(jnp.where(same, g, 0.0), axis=1, keepdims=True), g_row)
    decay = _each(lambda m, c, r: jnp.exp(jnp.where(m, c - r, -jnp.inf)), incl, gc_col, gc_row)
    qkk = _each(lambda qi, ki: _dot_nt(jnp.concatenate([qi, ki], axis=0), ki), q, k)
    lmat = _each(lambda m, b, x, dc: jnp.where(m, b * x[tm:] * dc, 0.0), strict, beta, qkk, decay)
    t_inv = _unit_tri_inverse(lmat, bi, bj)
    eg = _each(jnp.exp, gc_col)
    wu = _each(lambda t, ki, vi, b, e: _dot(t, jnp.concatenate([ki * (b * e), vi * b], axis=1)),
               t_inv, k, v, beta, eg)
    awu = _each(lambda x, dc, w: _dot(x[:tm] * (scale * dc), w), qkk, decay, wu)
    q_eff = _each(lambda qi, e, a: qi * (scale * e) - a[:, :dk], q, eg, awu)
    k_tail = _each(lambda ki, t, c: ki * jnp.exp(t - c), k, g_tot, gc_col)
    s = [s_refs[d][hh] for hh, d in probs]
    for step in range(n_chunks):
        for p, (hh, d) in enumerate(probs):
            c = step if d == 0 else n_chunks - 1 - step
            sl = slice(c * cs, (c + 1) * cs)
            kwu = _dot_tn(k_tail[p][sl], wu[p][sl])
            o_refs[d][sl, hh * dk:(hh + 1) * dk] = (
                _dot(q_eff[p][sl], s[p]) + awu[p][sl, dk:]).astype(o_refs[d].dtype)
            s[p] = s[p] * jnp.exp(g_tot[p][c * cs:c * cs + 1, :]) + (kwu[:, dk:] - _dot(kwu[:, :dk], s[p]))
    for p, (hh, d) in enumerate(probs):
        s_refs[d][hh] = s[p]


def _gdn_out_kernel(of_ref, ob_ref, z_ref, x_ref, m_ref, ng_ref, w_ref, g2_ref, wrh_ref, wrl_ref, br_ref,
                    xo_ref, h2_ref, route_ref, cnt_ref):
    d = x_ref.shape[-1]
    m = m_ref[...]
    o = of_ref[...].astype(F32) + ob_ref[...].astype(F32)
    z = z_ref[...].astype(F32)
    ng = ng_ref[...]
    parts = []
    for hd in range(o.shape[1] // LANES):
        sl = slice(hd * LANES, (hd + 1) * LANES)
        parts.append(_rms_scale(o[:, sl]) * ng * _silu(z[:, sl]))
    y = jnp.dot(jnp.concatenate(parts, axis=1).astype(BF16), w_ref[...], preferred_element_type=F32)
    x_new = x_ref[...] + _mod_part(m, 2, d) * y
    xo_ref[...] = x_new
    _route_epilogue(x_new, m, g2_ref[...], wrh_ref[...], wrl_ref[...], br_ref[...],
                    _strict_lower(x_new.shape[0]), h2_ref, route_ref, cnt_ref)


def _gmlp_kernel(x_ref, m_ref, g1_ref, win_ref, bin_ref, lng_ref, lnb_ref, ws_ref, bs_ref, wout_ref,
                 g2_ref, wrh_ref, wrl_ref, br_ref, xo_ref, h2_ref, route_ref, cnt_ref):
    d = x_ref.shape[-1]
    tm = x_ref.shape[0]
    m = m_ref[...]
    x = x_ref[...]
    h = (_rms_scale(x) * g1_ref[...]) * (1.0 + _mod_part(m, 1, d)) + _mod_part(m, 0, d)
    zz = jnp.dot(h.astype(BF16), win_ref[...], preferred_element_type=F32) + bin_ref[...]
    zz = 0.5 * zz * (1.0 + lax.erf(zz * (2.0 ** -0.5)))
    width = zz.shape[1] // 2
    u, v = zz[:, :width], zz[:, width:]
    mu = jnp.mean(v, axis=-1, keepdims=True)
    vc = v - mu
    v = (vc * lax.rsqrt(jnp.mean(vc * vc, axis=-1, keepdims=True) + NORM_EPS)) * lng_ref[...] + lnb_ref[...]
    vb = v.astype(BF16)
    gw = width // GMLP_GROUPS
    rows = []
    for c in range(tm // GMLP_CHUNK):
        rs = slice(c * GMLP_CHUNK, (c + 1) * GMLP_CHUNK)
        cols = []
        for g in range(GMLP_GROUPS):
            sg = jnp.dot(ws_ref[g], vb[rs, g * gw:(g + 1) * gw], preferred_element_type=F32)
            cols.append(sg + bs_ref[:, g:g + 1])
        rows.append(jnp.concatenate(cols, axis=1))
    s = jnp.concatenate(rows, axis=0)
    y = jnp.dot((u * s).astype(BF16), wout_ref[...], preferred_element_type=F32)
    x_new = x + _mod_part(m, 2, d) * y
    xo_ref[...] = x_new
    _route_epilogue(x_new, m, g2_ref[...], wrh_ref[...], wrl_ref[...], br_ref[...],
                    _strict_lower(tm), h2_ref, route_ref, cnt_ref)


def _local_positions(route, loffv):
    tm = route.shape[0]
    lane = lax.broadcasted_iota(jnp.int32, (tm, LANES), 1).astype(F32)
    out = []
    for k in range(TOP_K):
        first = jnp.sum(jnp.where(lane == route[:, k:k + 1], loffv, 0.0), axis=1, keepdims=True)
        out.append(first + route[:, 2 * TOP_K + k:2 * TOP_K + k + 1])
    return out


def _rows(first, count):
    return pl.ds(pl.multiple_of(first * SUBLANES, SUBLANES), count * SUBLANES)


def _lane_block(first, count, sb):
    return pl.ds(first * SUBLANES + sb, count, stride=SUBLANES)


def _run_pieces(cnt, fn):
    size = TOK_TILE
    while size >= 1:
        def piece(size=size):
            fn(cnt & ~(2 * size - 1), size)
        pl.when((cnt & size) != 0)(piece)
        size //= 2


def _dispatch_kernel(cnt_s, loff_s, base_s, pstart_s, plen_s, h_ref, route_ref, loffv_ref, o_ref, buf_ref, sem):
    i = pl.program_id(0)
    n_exp = plen_s.shape[0]
    tm = h_ref.shape[0]
    n_pairs = tm * TOP_K
    lpos = _local_positions(route_ref[...], loffv_ref[...])
    ii = lax.broadcasted_iota(jnp.int32, (tm, tm), 0)
    jj = lax.broadcasted_iota(jnp.int32, (tm, tm), 1)
    lpos_rows = [jnp.sum(jnp.where(ii == jj, lp, 0.0), axis=0, keepdims=True) for lp in lpos]
    hb = h_ref[...].astype(BF16)
    for blk in range(TOP_K):
        pair = (ii + blk * tm).astype(F32)
        perm = jnp.zeros((tm, tm), F32)
        for k in range(TOP_K):
            perm = perm + (pair == lpos_rows[k]).astype(F32)
        hp = jnp.dot(perm.astype(BF16), hb, preferred_element_type=F32)
        for sb in range(hp.shape[1] // LANES):
            buf_ref[_lane_block(blk * tm, tm, sb), :] = hp[:, sb * LANES:(sb + 1) * LANES]

    def start_run(e, carry):
        lo, bs = loff_s[i * n_exp + e], base_s[i * n_exp + e]
        _run_pieces(cnt_s[i * n_exp + e], lambda off, size: pltpu.make_async_copy(
            buf_ref.at[_rows(lo + off, size)], o_ref.at[_rows(bs + off, size)], sem).start())
        return carry

    lax.fori_loop(0, n_exp, start_run, 0)
    pltpu.make_async_copy(buf_ref, o_ref.at[_rows(0, n_pairs)], sem).wait()

    @pl.when(i == pl.num_programs(0) - 1)
    def _():
        buf_ref[0:MOE_TILE * SUBLANES, :] = jnp.zeros((MOE_TILE * SUBLANES, LANES), F32)

        def pad_copy(e, off, size):
            return pltpu.make_async_copy(buf_ref.at[_rows(0, size)], o_ref.at[_rows(pstart_s[e] + off, size)], sem)

        def fill(e, carry):
            _run_pieces(plen_s[e], lambda off, size: pad_copy(e, off, size).start())
            _run_pieces(plen_s[e], lambda off, size: pad_copy(e, off, size).wait())
            return carry

        lax.fori_loop(0, n_exp, fill, 0)


def _expert_kernel(te_ref, nt_ref, x_ref, wgu_ref, bgu_ref, wdn_ref, bdn_ref, o_ref, xs_ref):
    del te_ref
    n_rows = xs_ref.shape[0]

    @pl.when(pl.program_id(0) < nt_ref[0])
    def _():
        de = wdn_ref.shape[0]
        for sb in range(SUBLANES):
            xs_ref[:, sb * LANES:(sb + 1) * LANES] = x_ref[_lane_block(0, n_rows, sb), :].astype(BF16)
        gu = jnp.dot(xs_ref[...], wgu_ref[...], preferred_element_type=F32) + bgu_ref[...]
        gate = jnp.minimum(gu[:, :de], SWIGLU_LIMIT)
        up = jnp.clip(gu[:, de:], -SWIGLU_LIMIT, SWIGLU_LIMIT)
        glu = gate * jax.nn.sigmoid(gate * SWIGLU_ALPHA)
        act = ((up + 1.0) * glu).astype(BF16)
        y = jnp.dot(act, wdn_ref[...], preferred_element_type=F32) + bdn_ref[...]
        for sb in range(SUBLANES):
            o_ref[_lane_block(0, n_rows, sb), :] = y[:, sb * LANES:(sb + 1) * LANES]

    @pl.when(pl.program_id(0) >= nt_ref[0])
    def _():
        o_ref[...] = jnp.zeros_like(o_ref)


def _combine_kernel(cnt_s, loff_s, base_s, x_ref, route_ref, loffv_ref, m_ref, y_ref, o_ref, buf_ref, sem):
    i = pl.program_id(0)
    n_exp = cnt_s.shape[0] // pl.num_programs(0)
    tm, d = x_ref.shape
    n_pairs = tm * TOP_K

    def start_run(e, carry):
        lo, bs = loff_s[i * n_exp + e], base_s[i * n_exp + e]
        _run_pieces(cnt_s[i * n_exp + e], lambda off, size: pltpu.make_async_copy(
            y_ref.at[_rows(bs + off, size)], buf_ref.at[_rows(lo + off, size)], sem).start())
        return carry

    lax.fori_loop(0, n_exp, start_run, 0)
    route = route_ref[...]
    lpos = _local_positions(route, loffv_ref[...])
    jj = lax.broadcasted_iota(jnp.int32, (tm, tm), 1)
    combs = []
    for blk in range(TOP_K):
        pair = (jj + blk * tm).astype(F32)
        comb = jnp.zeros((tm, tm), F32)
        for k in range(TOP_K):
            comb = comb + jnp.where(pair == lpos[k], route[:, TOP_K + k:TOP_K + k + 1], 0.0)
        combs.append(comb.astype(BF16))
    pltpu.make_async_copy(y_ref.at[_rows(0, n_pairs)], buf_ref, sem).wait()
    acc = jnp.zeros((tm, d), F32)
    for blk in range(TOP_K):
        y = jnp.concatenate([buf_ref[_lane_block(blk * tm, tm, sb), :].astype(BF16) for sb in range(SUBLANES)],
                            axis=1)
        acc = acc + jnp.dot(combs[blk], y, preferred_element_type=F32)
    o_ref[...] = x_ref[...] + _mod_part(m_ref[...], 5, d) * acc


def _final_kernel(x_ref, g_ref, o_ref):
    o_ref[...] = _rms_scale(x_ref[...]) * g_ref[...]


def _moe(xs, h2, route, counts, mods_l, mod_map, w_gu, b_gu, w_dn, b_dn, n_bt):
    b, s, d = xs.shape
    n = b * s
    n_exp = w_gu.shape[0]
    nt_tok = n // TOK_TILE
    max_tiles = (n * TOP_K) // MOE_TILE + n_exp
    n_pairs = TOK_TILE * TOP_K
    cnt = counts.reshape(nt_tok, LANES)[:, :n_exp].astype(jnp.int32)
    totals = jnp.sum(cnt, axis=0)
    tiles_e = (totals + MOE_TILE - 1) // MOE_TILE
    tile_end = jnp.cumsum(tiles_e)
    starts = (tile_end - tiles_e) * MOE_TILE
    base = starts[None, :] + jnp.cumsum(cnt, axis=0) - cnt
    loff = jnp.cumsum(cnt, axis=1) - cnt
    loffv = jnp.zeros((nt_tok, 1, LANES), F32).at[:, 0, :n_exp].set(loff.astype(F32))
    tile_ids = jnp.arange(max_tiles, dtype=jnp.int32)
    tile_expert = jnp.minimum(jnp.sum((tile_end[None, :] <= tile_ids[:, None]).astype(jnp.int32), axis=1),
                              n_exp - 1)
    n_tiles = tile_end[-1:].astype(jnp.int32)
    plan = (cnt.reshape(-1), loff.reshape(-1), base.reshape(-1))
    sorted_shape = jax.ShapeDtypeStruct((max_tiles * MOE_TILE * SUBLANES, LANES), F32)

    hs = pl.pallas_call(
        _dispatch_kernel,
        grid_spec=pltpu.PrefetchScalarGridSpec(
            num_scalar_prefetch=5,
            grid=(nt_tok,),
            in_specs=[pl.BlockSpec((TOK_TILE, d), lambda i, *_: (i, 0)),
                      pl.BlockSpec((TOK_TILE, LANES), lambda i, *_: (i, 0)),
                      pl.BlockSpec((None, 1, LANES), lambda i, *_: (i, 0, 0))],
            out_specs=pl.BlockSpec(memory_space=pl.ANY),
            scratch_shapes=[pltpu.VMEM((n_pairs * SUBLANES, LANES), F32), pltpu.SemaphoreType.DMA]),
        out_shape=sorted_shape,
        compiler_params=_cparams(("arbitrary",)),
        name="moe_dispatch",
    )(*plan, starts + totals, tiles_e * MOE_TILE - totals, h2.reshape(n, d), route.reshape(n, LANES), loffv)

    de = w_dn.shape[1]
    ys = pl.pallas_call(
        _expert_kernel,
        grid_spec=pltpu.PrefetchScalarGridSpec(
            num_scalar_prefetch=2,
            grid=(max_tiles,),
            in_specs=[pl.BlockSpec((MOE_TILE * SUBLANES, LANES), lambda j, te, nt: (jnp.where(j < nt[0], j, 0), 0)),
                      pl.BlockSpec((None, d, 2 * de), lambda j, te, nt: (te[j], 0, 0)),
                      pl.BlockSpec((None, 1, 2 * de), lambda j, te, nt: (te[j], 0, 0)),
                      pl.BlockSpec((None, de, d), lambda j, te, nt: (te[j], 0, 0)),
                      pl.BlockSpec((None, 1, d), lambda j, te, nt: (te[j], 0, 0))],
            out_specs=pl.BlockSpec((MOE_TILE * SUBLANES, LANES), lambda j, te, nt: (j, 0)),
            scratch_shapes=[pltpu.VMEM((MOE_TILE, d), BF16)]),
        out_shape=sorted_shape,
        compiler_params=_cparams(("arbitrary",)),
        name="moe_experts",
    )(tile_expert, n_tiles, hs, w_gu, b_gu.reshape(n_exp, 1, 2 * de), w_dn, b_dn.reshape(n_exp, 1, d))

    out = pl.pallas_call(
        _combine_kernel,
        grid_spec=pltpu.PrefetchScalarGridSpec(
            num_scalar_prefetch=3,
            grid=(nt_tok,),
            in_specs=[pl.BlockSpec((TOK_TILE, d), lambda i, *_: (i, 0)),
                      pl.BlockSpec((TOK_TILE, LANES), lambda i, *_: (i, 0)),
                      pl.BlockSpec((None, 1, LANES), lambda i, *_: (i, 0, 0)),
                      pl.BlockSpec((None, 1, N_ADA * d), lambda i, *_: (mod_map(i // n_bt, i % n_bt), 0, 0)),
                      pl.BlockSpec(memory_space=pl.ANY)],
            out_specs=pl.BlockSpec((TOK_TILE, d), lambda i, *_: (i, 0)),
            scratch_shapes=[pltpu.VMEM((n_pairs * SUBLANES, LANES), F32), pltpu.SemaphoreType.DMA]),
        out_shape=jax.ShapeDtypeStruct((n, d), F32),
        input_output_aliases={3: 0},
        compiler_params=_cparams(("arbitrary",)),
        name="moe_combine",
    )(*plan, xs.reshape(n, d), route.reshape(n, LANES), loffv, mods_l, ys)
    return out.reshape(b, s, d)


def kernel(x, c, ctx, c_ctx, ada_w, ada_b, norm1_g, norm2_g, gdn_w_in, gdn_conv_w, gdn_a_log, gdn_dt_bias, gdn_norm_g, gdn_w_out, gmlp_w_in, gmlp_b_in, gmlp_ln_g, gmlp_ln_b, gmlp_w_s, gmlp_b_s, gmlp_w_out, router_w, router_b, moe_w_gu, moe_b_gu, moe_w_dn, moe_b_dn, final_g):
    b, t, d = x.shape
    n_ctx = ctx.shape[1]
    s = n_ctx + t
    depth = ada_w.shape[0]
    n_exp = router_w.shape[-1]
    assert t % TOK_TILE == 0 and n_ctx % TOK_TILE == 0 and d % LANES == 0 and n_exp <= LANES
    assert (b * s * TOP_K) % MOE_TILE == 0 and d == SUBLANES * LANES
    n_bt = s // TOK_TILE
    n_ct = n_ctx // TOK_TILE
    qk_w = GDN_HEADS * GDN_DK
    qkv_w = 3 * qk_w

    n_rows = -(-(b + 1) // 8) * 8
    cc = jnp.zeros((n_rows, d), F32).at[:b].set(c).at[b].set(c_ctx)
    mods = _ada_mods(cc, ada_w, ada_b).reshape(depth, n_rows, 1, N_ADA * d)

    def mod_map(bi, ti):
        return jnp.where(ti < n_ct, b, bi)

    mod_spec = pl.BlockSpec((None, 1, N_ADA * d), lambda bi, ti: (mod_map(bi, ti), 0, 0))
    tok_spec = lambda w: pl.BlockSpec((None, TOK_TILE, w), lambda bi, ti: (bi, ti, 0))
    full = lambda shape: pl.BlockSpec(shape, lambda bi, ti: (0,) * len(shape))
    tok_shape = lambda w, dt: jax.ShapeDtypeStruct((b, s, w), dt)

    xs = jnp.concatenate([ctx, x], axis=1)

    route_out_specs = [tok_spec(d), tok_spec(d), tok_spec(LANES),
                       pl.BlockSpec((None, 1, LANES), lambda bi, ti: (bi * n_bt + ti, 0, 0))]
    route_out_shapes = [tok_shape(d, F32), tok_shape(d, F32), tok_shape(LANES, F32),
                        jax.ShapeDtypeStruct((b * n_bt, 1, LANES), F32)]

    for i in range(depth):
        j = i // 2
        mods_l = mods[i]
        g1 = norm1_g[i].reshape(1, d)
        g2 = norm2_g[i].reshape(1, d)
        wr = jnp.zeros((d, LANES), F32).at[:, :n_exp].set(router_w[i])
        wr_hi = wr.astype(BF16)
        wr_lo = (wr - wr_hi.astype(F32)).astype(BF16)
        br = jnp.full((1, LANES), -1e30, F32).at[0, :n_exp].set(router_b[i])
        if i % 2 == 0:
            w_in = gdn_w_in[j]
            w_main = w_in[:, :qkv_w + qk_w].astype(BF16)
            n_ab = w_in.shape[1] - (qkv_w + qk_w)
            w_ab = jnp.zeros((d, LANES), F32).at[:, :n_ab].set(w_in[:, qkv_w + qk_w:]).astype(BF16)
            ab_c = jnp.zeros((8, LANES), F32)
            ab_c = ab_c.at[0, :2 * GDN_HEADS].set(-jnp.exp(gdn_a_log[j].reshape(-1)))
            ab_c = ab_c.at[1, :2 * GDN_HEADS].set(gdn_dt_bias[j].reshape(-1))
            qkv, z, gb = pl.pallas_call(
                functools.partial(_gdn_in_kernel, qkv_w=qkv_w),
                grid=(b, n_bt),
                in_specs=[tok_spec(d), mod_spec, full((1, d)), full((d, qkv_w + qk_w)), full((d, LANES)),
                          full((8, LANES))],
                out_specs=[tok_spec(qkv_w), tok_spec(qk_w), tok_spec(LANES)],
                out_shape=[tok_shape(qkv_w, BF16), tok_shape(qk_w, BF16), tok_shape(LANES, F32)],
                compiler_params=_cparams(("parallel", "parallel")),
                name="gdn_in",
            )(xs, mods_l, g1, w_main, w_ab, ab_c)

            qkv_c = pl.pallas_call(
                functools.partial(_gdn_conv_kernel, n_ctx=n_ctx, n_lat=t, qk_blocks=2 * GDN_HEADS),
                grid=(b, qkv_w // LANES),
                in_specs=[pl.BlockSpec((None, s, LANES), lambda bi, ci: (bi, 0, ci)),
                          pl.BlockSpec((GDN_CONV, LANES), lambda bi, ci: (0, ci))],
                out_specs=pl.BlockSpec((None, s, LANES), lambda bi, ci: (bi, 0, ci)),
                out_shape=tok_shape(qkv_w, BF16),
                scratch_shapes=[pltpu.VMEM((s + 24, LANES), F32)],
                compiler_params=_cparams(("parallel", "parallel")),
                name="gdn_conv",
            )(qkv, gdn_conv_w[j])

            gbt = jnp.swapaxes(gb[:, :, :4 * GDN_HEADS], 1, 2)

            def bwd_tile(ti):
                return jnp.where(ti < n_ct, n_ct - 1 - ti, n_bt - 1 - (ti - n_ct))

            hw = SCAN_HEADS * GDN_DK
            n_hb = GDN_HEADS // SCAN_HEADS

            def hd_spec(off, tmap):
                return pl.BlockSpec((None, TOK_TILE, hw), lambda bi, hi, ti: (bi, tmap(ti), off + hi))

            fwd_tile = lambda ti: ti
            scan_in = []
            for tmap in (fwd_tile, bwd_tile):
                scan_in += [hd_spec(0, tmap), hd_spec(n_hb, tmap), hd_spec(2 * n_hb, tmap),
                            pl.BlockSpec((None, TOK_TILE, LANES), lambda bi, hi, ti, tmap=tmap: (bi, tmap(ti), 0)),
                            pl.BlockSpec((None, 4 * GDN_HEADS, TOK_TILE),
                                         lambda bi, hi, ti, tmap=tmap: (bi, 0, tmap(ti)))]
            o_f, o_b = pl.pallas_call(
                functools.partial(_gdn_scan_kernel, scale=GDN_DK ** -0.5),
                grid=(b, n_hb, n_bt),
                in_specs=scan_in,
                out_specs=[hd_spec(0, fwd_tile), hd_spec(0, bwd_tile)],
                out_shape=[tok_shape(qk_w, BF16), tok_shape(qk_w, BF16)],
                scratch_shapes=[pltpu.VMEM((SCAN_HEADS, GDN_DK, GDN_DK), F32),
                                pltpu.VMEM((SCAN_HEADS, GDN_DK, GDN_DK), F32)],
                compiler_params=_cparams(("parallel", "parallel", "arbitrary")),
                name="gdn_scan",
            )(qkv_c, qkv_c, qkv_c, gb, gbt, qkv_c, qkv_c, qkv_c, gb, gbt)

            xs, h2, route, counts = pl.pallas_call(
                _gdn_out_kernel,
                grid=(b, n_bt),
                in_specs=[tok_spec(qk_w), tok_spec(qk_w), tok_spec(qk_w), tok_spec(d), mod_spec,
                          full((1, LANES)), full((qk_w, d)), full((1, d)), full((d, LANES)), full((d, LANES)),
                          full((1, LANES))],
                out_specs=route_out_specs,
                out_shape=route_out_shapes,
                input_output_aliases={3: 0},
                compiler_params=_cparams(("parallel", "parallel")),
                name="gdn_out",
            )(o_f, o_b, z, xs, mods_l, gdn_norm_g[j].reshape(1, LANES), gdn_w_out[j].astype(BF16), g2,
              wr_hi, wr_lo, br)
        else:
            width = gmlp_w_out.shape[1]
            xs, h2, route, counts = pl.pallas_call(
                _gmlp_kernel,
                grid=(b, n_bt),
                in_specs=[tok_spec(d), mod_spec, full((1, d)), full((d, 2 * width)), full((1, 2 * width)),
                          full((1, width)), full((1, width)), full((GMLP_GROUPS, GMLP_CHUNK, GMLP_CHUNK)),
                          full((GMLP_CHUNK, GMLP_GROUPS)), full((width, d)), full((1, d)), full((d, LANES)),
                          full((d, LANES)), full((1, LANES))],
                out_specs=route_out_specs,
                out_shape=route_out_shapes,
                input_output_aliases={0: 0},
                compiler_params=_cparams(("parallel", "parallel")),
                name="gmlp",
            )(xs, mods_l, g1, gmlp_w_in[j].astype(BF16), gmlp_b_in[j].reshape(1, -1),
              gmlp_ln_g[j].reshape(1, -1), gmlp_ln_b[j].reshape(1, -1), gmlp_w_s[j].astype(BF16),
              gmlp_b_s[j].T, gmlp_w_out[j].astype(BF16), g2, wr_hi, wr_lo, br)

        xs = _moe(xs, h2, route, counts, mods_l, mod_map, moe_w_gu[i].astype(BF16), moe_b_gu[i],
                  moe_w_dn[i].astype(BF16), moe_b_dn[i], n_bt)

    return pl.pallas_call(
        _final_kernel,
        grid=(b, t // TOK_TILE),
        in_specs=[pl.BlockSpec((None, TOK_TILE, d), lambda bi, ti: (bi, n_ct + ti, 0)), full((1, d))],
        out_specs=pl.BlockSpec((None, TOK_TILE, d), lambda bi, ti: (bi, ti, 0)),
        out_shape=jax.ShapeDtypeStruct((b, t, d), x.dtype),
        compiler_params=_cparams(("parallel", "parallel")),
        name="final_norm",
    )(xs, final_g.reshape(1, d))
```

```python
import functools

import jax
import jax.numpy as jnp
from jax import lax
from jax.experimental import pallas as pl
from jax.experimental.pallas import tpu as pltpu

F32 = jnp.float32
BF16 = jnp.bfloat16

NORM_EPS = 1e-6
N_ADA = 6
GDN_HEADS = 8
GDN_DK = 128
GDN_CONV = 5
GDN_CHUNK = 64
GMLP_CHUNK = 128
GMLP_GROUPS = 8
TOP_K = 4
SWIGLU_ALPHA = 1.702
SWIGLU_LIMIT = 7.0

LANES = 128
SUBLANES = 8
TOK_TILE = 256
MOE_TILE = 512
INV_BLOCK = 16
SCAN_HEADS = 4
VMEM_LIMIT = 56 * 1024 * 1024


def _cparams(sem):
    return pltpu.CompilerParams(dimension_semantics=sem, vmem_limit_bytes=VMEM_LIMIT)


def _dot(a, b):
    return jnp.dot(a.astype(BF16), b.astype(BF16), preferred_element_type=F32)


def _dot_nt(a, b):
    return lax.dot_general(a.astype(BF16), b.astype(BF16), (((1,), (1,)), ((), ())),
                           preferred_element_type=F32)


def _dot_tn(a, b):
    return lax.dot_general(a.astype(BF16), b.astype(BF16), (((0,), (0,)), ((), ())),
                           preferred_element_type=F32)


def _split(a):
    hi = a.astype(BF16)
    lo = (a - hi.astype(F32)).astype(BF16)
    return hi, lo


def _dot3(a, b):
    ah, al = _split(a)
    bh, bl = _split(b)
    d = lambda x, y: jnp.dot(x, y, preferred_element_type=F32)
    return d(ah, bh) + (d(ah, bl) + d(al, bh))


def _rms_scale(xf):
    return xf * lax.rsqrt(jnp.mean(xf * xf, axis=-1, keepdims=True) + NORM_EPS)


def _silu(x):
    return x * jax.nn.sigmoid(x)


def _mod_part(m, k, d):
    return m[:, k * d:(k + 1) * d]


def _ada_kernel(cc_ref, w_ref, b_ref, o_ref):
    s = _silu(cc_ref[...])
    o_ref[...] = _dot3(s, w_ref[...]) + b_ref[...]


def _ada_mods(cc, ada_w, ada_b):
    depth, d, n = ada_w.shape
    r = cc.shape[0]
    tn = n // 4
    return pl.pallas_call(
        _ada_kernel,
        grid=(depth, n // tn),
        in_specs=[pl.BlockSpec((r, d), lambda l, j: (0, 0)),
                  pl.BlockSpec((None, d, tn), lambda l, j: (l, 0, j)),
                  pl.BlockSpec((None, 1, tn), lambda l, j: (l, 0, j))],
        out_specs=pl.BlockSpec((None, r, tn), lambda l, j: (l, 0, j)),
        out_shape=jax.ShapeDtypeStruct((depth, r, n), F32),
        compiler_params=_cparams(("parallel", "parallel")),
        name="ada_mods",
    )(cc, ada_w, ada_b.reshape(depth, 1, n))


def _route_epilogue(x_new, m, g2, wr_hi, wr_lo, br, tri, h2_ref, route_ref, cnt_ref):
    d = x_new.shape[-1]
    tm = x_new.shape[0]
    h2 = (_rms_scale(x_new) * g2) * (1.0 + _mod_part(m, 4, d)) + _mod_part(m, 3, d)
    h2_ref[...] = h2
    hh, hl = _split(h2)
    dd = lambda a, b: jnp.dot(a, b, preferred_element_type=F32)
    logits = dd(hh, wr_hi) + (dd(hh, wr_lo) + dd(hl, wr_hi)) + br
    lane = lax.broadcasted_iota(jnp.int32, (tm, LANES), 1).astype(F32)
    vals, idxs = [], []
    l = logits
    for _ in range(TOP_K):
        mx = jnp.max(l, axis=1, keepdims=True)
        ix = jnp.min(jnp.where(l == mx, lane, float(LANES)), axis=1, keepdims=True)
        vals.append(mx)
        idxs.append(ix)
        l = jnp.where(lane == ix, -jnp.inf, l)
    es = [jnp.exp(v - vals[0]) for v in vals]
    den = es[0] + es[1] + es[2] + es[3]
    hots = [(lane == ix).astype(F32) for ix in idxs]
    oh4 = hots[0] + hots[1] + hots[2] + hots[3]
    prefix = jnp.dot(tri, oh4.astype(BF16), preferred_element_type=F32)
    route = jnp.zeros((tm, LANES), F32)
    for k in range(TOP_K):
        rank = jnp.sum(hots[k] * prefix, axis=1, keepdims=True)
        route = jnp.where(lane == float(k), idxs[k], route)
        route = jnp.where(lane == float(TOP_K + k), es[k] / den, route)
        route = jnp.where(lane == float(2 * TOP_K + k), rank, route)
    route_ref[...] = route
    cnt_ref[...] = jnp.sum(oh4, axis=0, keepdims=True)


def _strict_lower(n):
    i = lax.broadcasted_iota(jnp.int32, (n, n), 0)
    j = lax.broadcasted_iota(jnp.int32, (n, n), 1)
    return (j < i).astype(BF16)


def _gdn_in_kernel(x_ref, m_ref, g1_ref, w_ref, wab_ref, ab_ref, qkv_ref, z_ref, gb_ref, *, qkv_w):
    d = x_ref.shape[-1]
    m = m_ref[...]
    h = (_rms_scale(x_ref[...]) * g1_ref[...]) * (1.0 + _mod_part(m, 1, d)) + _mod_part(m, 0, d)
    hb = h.astype(BF16)
    p = jnp.dot(hb, w_ref[...], preferred_element_type=F32)
    qkv_ref[...] = p[:, :qkv_w].astype(BF16)
    z_ref[...] = p[:, qkv_w:].astype(BF16)
    ab = jnp.dot(hb, wab_ref[...], preferred_element_type=F32)
    neg_a = ab_ref[0:1, :]
    dt_b = ab_ref[1:2, :]
    xs = ab + dt_b
    softplus = jnp.maximum(xs, 0.0) + jnp.log(1.0 + jnp.exp(-jnp.abs(xs)))
    lane = lax.broadcasted_iota(jnp.int32, ab.shape, 1)
    gb_ref[...] = jnp.where(lane < 2 * GDN_HEADS, neg_a * softplus, jax.nn.sigmoid(ab))


def _gdn_conv_kernel(x_ref, w_ref, o_ref, pad_ref, *, n_ctx, n_lat, qk_blocks):
    tile = TOK_TILE
    zeros8 = jnp.zeros((8, LANES), F32)
    pad_ref[0:8, :] = zeros8
    pad_ref[8 + n_ctx:16 + n_ctx, :] = zeros8
    pad_ref[16 + n_ctx + n_lat:24 + n_ctx + n_lat, :] = zeros8
    for r in range(0, n_ctx + n_lat, tile):
        base = 8 + r if r < n_ctx else 16 + r
        pad_ref[base:base + tile, :] = x_ref[r:r + tile, :].astype(F32)
    w = w_ref[...]
    is_qk = pl.program_id(1) < qk_blocks
    half = GDN_CONV // 2
    for r in range(0, n_ctx + n_lat, tile):
        base = 8 + r if r < n_ctx else 16 + r
        acc = jnp.zeros((tile, LANES), F32)
        for k in range(GDN_CONV):
            s = base + k - half
            acc = acc + pad_ref[s:s + tile, :] * w[k:k + 1, :]
        y = _silu(acc)
        yn = y * lax.rsqrt(jnp.sum(y * y, axis=-1, keepdims=True) + NORM_EPS)
        o_ref[r:r + tile, :] = jnp.where(is_qk, yn, y).astype(BF16)


def _each(fn, *seqs):
    return [fn(*args) for args in zip(*seqs)]


def _unit_tri_inverse(lmats, bi, bj):
    n = lmats[0].shape[0]
    eye = (lax.broadcasted_iota(jnp.int32, (n, n), 0) == lax.broadcasted_iota(jnp.int32, (n, n), 1)).astype(F32)
    ld = _each(lambda l: jnp.where(bi == bj, l, 0.0), lmats)
    x = _each(lambda l: eye - l, ld)
    p = _each(_dot, ld, ld)
    width = 2
    while True:
        x = _each(lambda xi, pi: xi + _dot(xi, pi), x, p)
        width *= 2
        if width >= INV_BLOCK:
            break
        p = _each(_dot, p, p)
    span = 2
    while INV_BLOCK * span <= GDN_CHUNK:
        half = span // 2
        off = ((bi // span) == (bj // span)) & ((bi // half) != (bj // half))
        xc = _each(lambda xi, l: _dot(xi, jnp.where(off, l, 0.0)), x, lmats)
        x = _each(lambda xi, xci: xi - _dot(xci, xi), x, xc)
        span *= 2
    return x


def _gdn_scan_kernel(qf_ref, kf_ref, vf_ref, gbf_ref, gtf_ref, qb_ref, kb_ref, vb_ref, gbb_ref, gtb_ref,
                     of_ref, ob_ref, sf_ref, sb_ref, *, scale):
    h0 = pl.program_id(1) * (qf_ref.shape[1] // GDN_DK)

    @pl.when(pl.program_id(2) == 0)
    def _():
        sf_ref[...] = jnp.zeros_like(sf_ref)
        sb_ref[...] = jnp.zeros_like(sb_ref)

    tm = qf_ref.shape[0]
    dk = GDN_DK
    hb = qf_ref.shape[1] // dk
    cs = GDN_CHUNK
    n_chunks = tm // cs
    lane = lax.broadcasted_iota(jnp.int32, (tm, LANES), 1)
    ii = lax.broadcasted_iota(jnp.int32, (tm, tm), 0)
    jj = lax.broadcasted_iota(jnp.int32, (tm, tm), 1)
    same = (ii // cs) == (jj // cs)
    bi, bj = ii // INV_BLOCK, jj // INV_BLOCK

    def col(gb, idx):
        return jnp.sum(jnp.where(lane == idx, gb, 0.0), axis=1, keepdims=True)

    probs = [(hh, d) for hh in range(hb) for d in range(2)]
    tri_incl = [same & (ii >= jj), same & (ii <= jj)]
    tri_strict = [same & (ii > jj), same & (ii < jj)]
    incl = [tri_incl[d] for _, d in probs]
    incl_t = [tri_incl[1 - d] for _, d in probs]
    strict = [tri_strict[d] for _, d in probs]
    gbs = [gbf_ref[...], gbb_ref[...]]
    gts = [gtf_ref, gtb_ref]
    qkv_refs = [(qf_ref, kf_ref, vf_ref), (qb_ref, kb_ref, vb_ref)]
    o_refs = [of_ref, ob_ref]
    s_refs = [sf_ref, sb_ref]
    g_col = [col(gbs[d], d * GDN_HEADS + h0 + hh) for hh, d in probs]
    beta = [col(gbs[d], (2 + d) * GDN_HEADS + h0 + hh) for hh, d in probs]
    g_row = [gts[d][pl.ds(d * GDN_HEADS + h0 + hh, 1), :] for hh, d in probs]
    q, k, v = [[qkv_refs[d][j][:, hh * dk:(hh + 1) * dk].astype(F32) for hh, d in probs] for j in range(3)]

    gc_col = _each(lambda m, g: jnp.sum(jnp.where(m, g, 0.0), axis=1, keepdims=True), incl, g_row)
    gc_row = _each(lambda m, g: jnp.sum(jnp.where(m, g, 0.0), axis=0, keepdims=True), incl_t, g_col)
    g_tot = _each(lambda g: jnp.sum(jnp.where(same, g, 0.0), axis=1, keepdims=True), g_row)
    decay = _each(lambda m, c, r: jnp.exp(jnp.where(m, c - r, -jnp.inf)), incl, gc_col, gc_row)
    qkk = _each(lambda qi, ki: _dot_nt(jnp.concatenate([qi, ki], axis=0), ki), q, k)
    lmat = _each(lambda m, b, x, dc: jnp.where(m, b * x[tm:] * dc, 0.0), strict, beta, qkk, decay)
    t_inv = _unit_tri_inverse(lmat, bi, bj)
    eg = _each(jnp.exp, gc_col)
    wu = _each(lambda t, ki, vi, b, e: _dot(t, jnp.concatenate([ki * (b * e), vi * b], axis=1)),
               t_inv, k, v, beta, eg)
    awu = _each(lambda x, dc, w: _dot(x[:tm] * (scale * dc), w), qkk, decay, wu)
    q_eff = _each(lambda qi, e, a: qi * (scale * e) - a[:, :dk], q, eg, awu)
    k_tail = _each(lambda ki, t, c: ki * jnp.exp(t - c), k, g_tot, gc_col)
    s = [s_refs[d][hh] for hh, d in probs]
    for step in range(n_chunks):
        for p, (hh, d) in enumerate(probs):
            c = step if d == 0 else n_chunks - 1 - step
            sl = slice(c * cs, (c + 1) * cs)
            kwu = _dot_tn(k_tail[p][sl], wu[p][sl])
            o_refs[d][sl, hh * dk:(hh + 1) * dk] = (
                _dot(q_eff[p][sl], s[p]) + awu[p][sl, dk:]).astype(o_refs[d].dtype)
            s[p] = s[p] * jnp.exp(g_tot[p][c * cs:c * cs + 1, :]) + (kwu[:, dk:] - _dot(kwu[:, :dk], s[p]))
    for p, (hh, d) in enumerate(probs):
        s_refs[d][hh] = s[p]


def _gdn_out_kernel(of_ref, ob_ref, z_ref, x_ref, m_ref, ng_ref, w_ref, g2_ref, wrh_ref, wrl_ref, br_ref,
                    xo_ref, h2_ref, route_ref, cnt_ref):
    d = x_ref.shape[-1]
    m = m_ref[...]
    o = of_ref[...].astype(F32) + ob_ref[...].astype(F32)
    z = z_ref[...].astype(F32)
    ng = ng_ref[...]
    parts = []
    for hd in range(o.shape[1] // LANES):
        sl = slice(hd * LANES, (hd + 1) * LANES)
        parts.append(_rms_scale(o[:, sl]) * ng * _silu(z[:, sl]))
    y = jnp.dot(jnp.concatenate(parts, axis=1).astype(BF16), w_ref[...], preferred_element_type=F32)
    x_new = x_ref[...] + _mod_part(m, 2, d) * y
    xo_ref[...] = x_new
    _route_epilogue(x_new, m, g2_ref[...], wrh_ref[...], wrl_ref[...], br_ref[...],
                    _strict_lower(x_new.shape[0]), h2_ref, route_ref, cnt_ref)


def _gmlp_kernel(x_ref, m_ref, g1_ref, win_ref, bin_ref, lng_ref, lnb_ref, ws_ref, bs_ref, wout_ref,
                 g2_ref, wrh_ref, wrl_ref, br_ref, xo_ref, h2_ref, route_ref, cnt_ref):
    d = x_ref.shape[-1]
    tm = x_ref.shape[0]
    m = m_ref[...]
    x = x_ref[...]
    h = (_rms_scale(x) * g1_ref[...]) * (1.0 + _mod_part(m, 1, d)) + _mod_part(m, 0, d)
    zz = jnp.dot(h.astype(BF16), win_ref[...], preferred_element_type=F32) + bin_ref[...]
    zz = 0.5 * zz * (1.0 + lax.erf(zz * (2.0 ** -0.5)))
    width = zz.shape[1] // 2
    u, v = zz[:, :width], zz[:, width:]
    mu = jnp.mean(v, axis=-1, keepdims=True)
    vc = v - mu
    v = (vc * lax.rsqrt(jnp.mean(vc * vc, axis=-1, keepdims=True) + NORM_EPS)) * lng_ref[...] + lnb_ref[...]
    vb = v.astype(BF16)
    gw = width // GMLP_GROUPS
    rows = []
    for c in range(tm // GMLP_CHUNK):
        rs = slice(c * GMLP_CHUNK, (c + 1) * GMLP_CHUNK)
        cols = []
        for g in range(GMLP_GROUPS):
            sg = jnp.dot(ws_ref[g], vb[rs, g * gw:(g + 1) * gw], preferred_element_type=F32)
            cols.append(sg + bs_ref[:, g:g + 1])
        rows.append(jnp.concatenate(cols, axis=1))
    s = jnp.concatenate(rows, axis=0)
    y = jnp.dot((u * s).astype(BF16), wout_ref[...], preferred_element_type=F32)
    x_new = x + _mod_part(m, 2, d) * y
    xo_ref[...] = x_new
    _route_epilogue(x_new, m, g2_ref[...], wrh_ref[...], wrl_ref[...], br_ref[...],
                    _strict_lower(tm), h2_ref, route_ref, cnt_ref)


def _local_positions(route, loffv):
    tm = route.shape[0]
    lane = lax.broadcasted_iota(jnp.int32, (tm, LANES), 1).astype(F32)
    out = []
    for k in range(TOP_K):
        first = jnp.sum(jnp.where(lane == route[:, k:k + 1], loffv, 0.0), axis=1, keepdims=True)
        out.append(first + route[:, 2 * TOP_K + k:2 * TOP_K + k + 1])
    return out


def _rows(first, count):
    return pl.ds(pl.multiple_of(first * SUBLANES, SUBLANES), count * SUBLANES)


def _lane_block(first, count, sb):
    return pl.ds(first * SUBLANES + sb, count, stride=SUBLANES)


def _run_pieces(cnt, fn):
    size = TOK_TILE
    while size >= 1:
        def piece(size=size):
            fn(cnt & ~(2 * size - 1), size)
        pl.when((cnt & size) != 0)(piece)
        size //= 2


def _dispatch_kernel(cnt_s, loff_s, base_s, pstart_s, plen_s, h_ref, route_ref, loffv_ref, o_ref, buf_ref, sem):
    i = pl.program_id(0)
    n_exp = plen_s.shape[0]
    tm = h_ref.shape[0]
    n_pairs = tm * TOP_K
    slot = i % 2
    buf = buf_ref.at[slot]
    lpos = _local_positions(route_ref[...], loffv_ref[...])
    ii = lax.broadcasted_iota(jnp.int32, (tm, tm), 0)
    jj = lax.broadcasted_iota(jnp.int32, (tm, tm), 1)
    lpos_rows = [jnp.sum(jnp.where(ii == jj, lp, 0.0), axis=0, keepdims=True) for lp in lpos]
    hb = h_ref[...].astype(BF16)
    for blk in range(TOP_K):
        pair = (ii + blk * tm).astype(F32)
        perm = jnp.zeros((tm, tm), F32)
        for k in range(TOP_K):
            perm = perm + (pair == lpos_rows[k]).astype(F32)
        hp = jnp.dot(perm.astype(BF16), hb, preferred_element_type=F32)
        for sb in range(hp.shape[1] // LANES):
            buf[_lane_block(blk * tm, tm, sb), :] = hp[:, sb * LANES:(sb + 1) * LANES]

    def start_run(e, carry):
        lo, bs = loff_s[i * n_exp + e], base_s[i * n_exp + e]
        _run_pieces(cnt_s[i * n_exp + e], lambda off, size: pltpu.make_async_copy(
            buf.at[_rows(lo + off, size)], o_ref.at[_rows(bs + off, size)], sem.at[slot]).start())
        return carry

    lax.fori_loop(0, n_exp, start_run, 0)

    def wait_tile(sl):
        pltpu.make_async_copy(buf_ref.at[sl], o_ref.at[_rows(0, n_pairs)], sem.at[sl]).wait()

    pl.when(i > 0)(lambda: wait_tile(1 - slot))

    @pl.when(i == pl.num_programs(0) - 1)
    def _():
        wait_tile(slot)
        buf[0:MOE_TILE * SUBLANES, :] = jnp.zeros((MOE_TILE * SUBLANES, LANES), F32)

        def pad_copy(e, off, size):
            return pltpu.make_async_copy(buf.at[_rows(0, size)], o_ref.at[_rows(pstart_s[e] + off, size)],
                                         sem.at[slot])

        def fill(e, carry):
            _run_pieces(plen_s[e], lambda off, size: pad_copy(e, off, size).start())
            _run_pieces(plen_s[e], lambda off, size: pad_copy(e, off, size).wait())
            return carry

        lax.fori_loop(0, n_exp, fill, 0)


def _expert_kernel(te_ref, nt_ref, x_ref, wgu_ref, bgu_ref, wdn_ref, bdn_ref, o_ref, xs_ref, wgu_bf, wdn_bf):
    j = pl.program_id(0)
    n_rows = xs_ref.shape[0]

    @pl.when((j == 0) | (te_ref[j] != te_ref[jnp.maximum(j - 1, 0)]))
    def _():
        wgu_bf[...] = wgu_ref[...].astype(BF16)
        wdn_bf[...] = wdn_ref[...].astype(BF16)

    @pl.when(j < nt_ref[0])
    def _():
        de = wdn_ref.shape[0]
        for sb in range(SUBLANES):
            xs_ref[:, sb * LANES:(sb + 1) * LANES] = x_ref[_lane_block(0, n_rows, sb), :].astype(BF16)
        gu = jnp.dot(xs_ref[...], wgu_bf[...], preferred_element_type=F32) + bgu_ref[...]
        gate = jnp.minimum(gu[:, :de], SWIGLU_LIMIT)
        up = jnp.clip(gu[:, de:], -SWIGLU_LIMIT, SWIGLU_LIMIT)
        glu = gate * jax.nn.sigmoid(gate * SWIGLU_ALPHA)
        act = ((up + 1.0) * glu).astype(BF16)
        y = jnp.dot(act, wdn_bf[...], preferred_element_type=F32) + bdn_ref[...]
        for sb in range(SUBLANES):
            o_ref[_lane_block(0, n_rows, sb), :] = y[:, sb * LANES:(sb + 1) * LANES]

    @pl.when(pl.program_id(0) >= nt_ref[0])
    def _():
        o_ref[...] = jnp.zeros_like(o_ref)


def _combine_kernel(cnt_s, loff_s, base_s, x_ref, route_ref, loffv_ref, m_ref, y_ref, o_ref, buf_ref, sem):
    i = pl.program_id(0)
    n_exp = cnt_s.shape[0] // pl.num_programs(0)
    tm, d = x_ref.shape
    n_pairs = tm * TOP_K
    slot = i % 2

    def fetch(tile, sl):
        def start_run(e, carry):
            lo, bs = loff_s[tile * n_exp + e], base_s[tile * n_exp + e]
            _run_pieces(cnt_s[tile * n_exp + e], lambda off, size: pltpu.make_async_copy(
                y_ref.at[_rows(bs + off, size)], buf_ref.at[sl].at[_rows(lo + off, size)], sem.at[sl]).start())
            return carry

        lax.fori_loop(0, n_exp, start_run, 0)

    pl.when(i == 0)(lambda: fetch(i, slot))
    pl.when(i + 1 < pl.num_programs(0))(lambda: fetch(i + 1, 1 - slot))
    buf = buf_ref.at[slot]
    route = route_ref[...]
    lpos = _local_positions(route, loffv_ref[...])
    jj = lax.broadcasted_iota(jnp.int32, (tm, tm), 1)
    combs = []
    for blk in range(TOP_K):
        pair = (jj + blk * tm).astype(F32)
        comb = jnp.zeros((tm, tm), F32)
        for k in range(TOP_K):
            comb = comb + jnp.where(pair == lpos[k], route[:, TOP_K + k:TOP_K + k + 1], 0.0)
        combs.append(comb.astype(BF16))
    pltpu.make_async_copy(y_ref.at[_rows(0, n_pairs)], buf, sem.at[slot]).wait()
    acc = jnp.zeros((tm, d), F32)
    for blk in range(TOP_K):
        y = jnp.concatenate([buf[_lane_block(blk * tm, tm, sb), :].astype(BF16) for sb in range(SUBLANES)],
                            axis=1)
        acc = acc + jnp.dot(combs[blk], y, preferred_element_type=F32)
    o_ref[...] = x_ref[...] + _mod_part(m_ref[...], 5, d) * acc


def _final_kernel(x_ref, g_ref, o_ref):
    o_ref[...] = _rms_scale(x_ref[...]) * g_ref[...]


def _moe(xs, h2, route, counts, mods_l, mod_map, layer, w_gu, b_gu, w_dn, b_dn, n_bt):
    b, s, d = xs.shape
    n = b * s
    n_exp = w_gu.shape[1]
    nt_tok = n // TOK_TILE
    max_tiles = (n * TOP_K) // MOE_TILE + n_exp
    n_pairs = TOK_TILE * TOP_K
    cnt = counts.reshape(nt_tok, LANES)[:, :n_exp].astype(jnp.int32)
    totals = jnp.sum(cnt, axis=0)
    tiles_e = (totals + MOE_TILE - 1) // MOE_TILE
    tile_end = jnp.cumsum(tiles_e)
    starts = (tile_end - tiles_e) * MOE_TILE
    base = starts[None, :] + jnp.cumsum(cnt, axis=0) - cnt
    loff = jnp.cumsum(cnt, axis=1) - cnt
    loffv = jnp.zeros((nt_tok, 1, LANES), F32).at[:, 0, :n_exp].set(loff.astype(F32))
    tile_ids = jnp.arange(max_tiles, dtype=jnp.int32)
    tile_expert = jnp.minimum(jnp.sum((tile_end[None, :] <= tile_ids[:, None]).astype(jnp.int32), axis=1),
                              n_exp - 1)
    n_tiles = tile_end[-1:].astype(jnp.int32)
    plan = (cnt.reshape(-1), loff.reshape(-1), base.reshape(-1))
    sorted_shape = jax.ShapeDtypeStruct((max_tiles * MOE_TILE * SUBLANES, LANES), F32)

    hs = pl.pallas_call(
        _dispatch_kernel,
        grid_spec=pltpu.PrefetchScalarGridSpec(
            num_scalar_prefetch=5,
            grid=(nt_tok,),
            in_specs=[pl.BlockSpec((TOK_TILE, d), lambda i, *_: (i, 0)),
                      pl.BlockSpec((TOK_TILE, LANES), lambda i, *_: (i, 0)),
                      pl.BlockSpec((None, 1, LANES), lambda i, *_: (i, 0, 0))],
            out_specs=pl.BlockSpec(memory_space=pl.ANY),
            scratch_shapes=[pltpu.VMEM((2, n_pairs * SUBLANES, LANES), F32), pltpu.SemaphoreType.DMA((2,))]),
        out_shape=sorted_shape,
        compiler_params=_cparams(("arbitrary",)),
        name="moe_dispatch",
    )(*plan, starts + totals, tiles_e * MOE_TILE - totals, h2.reshape(n, d), route.reshape(n, LANES), loffv)

    de = w_dn.shape[2]
    ys = pl.pallas_call(
        _expert_kernel,
        grid_spec=pltpu.PrefetchScalarGridSpec(
            num_scalar_prefetch=2,
            grid=(max_tiles,),
            in_specs=[pl.BlockSpec((MOE_TILE * SUBLANES, LANES), lambda j, te, nt: (jnp.where(j < nt[0], j, 0), 0)),
                      pl.BlockSpec((None, None, d, 2 * de), lambda j, te, nt: (layer, te[j], 0, 0)),
                      pl.BlockSpec((None, None, 1, 2 * de), lambda j, te, nt: (layer, te[j], 0, 0)),
                      pl.BlockSpec((None, None, de, d), lambda j, te, nt: (layer, te[j], 0, 0)),
                      pl.BlockSpec((None, None, 1, d), lambda j, te, nt: (layer, te[j], 0, 0))],
            out_specs=pl.BlockSpec((MOE_TILE * SUBLANES, LANES), lambda j, te, nt: (j, 0)),
            scratch_shapes=[pltpu.VMEM((MOE_TILE, d), BF16), pltpu.VMEM((d, 2 * de), BF16),
                            pltpu.VMEM((de, d), BF16)]),
        out_shape=sorted_shape,
        compiler_params=_cparams(("arbitrary",)),
        name="moe_experts",
    )(tile_expert, n_tiles, hs, w_gu, b_gu.reshape(-1, n_exp, 1, 2 * de), w_dn, b_dn.reshape(-1, n_exp, 1, d))

    out = pl.pallas_call(
        _combine_kernel,
        grid_spec=pltpu.PrefetchScalarGridSpec(
            num_scalar_prefetch=3,
            grid=(nt_tok,),
            in_specs=[pl.BlockSpec((TOK_TILE, d), lambda i, *_: (i, 0)),
                      pl.BlockSpec((TOK_TILE, LANES), lambda i, *_: (i, 0)),
                      pl.BlockSpec((None, 1, LANES), lambda i, *_: (i, 0, 0)),
                      pl.BlockSpec((None, 1, N_ADA * d), lambda i, *_: (mod_map(i // n_bt, i % n_bt), 0, 0)),
                      pl.BlockSpec(memory_space=pl.ANY)],
            out_specs=pl.BlockSpec((TOK_TILE, d), lambda i, *_: (i, 0)),
            scratch_shapes=[pltpu.VMEM((2, n_pairs * SUBLANES, LANES), F32), pltpu.SemaphoreType.DMA((2,))]),
        out_shape=jax.ShapeDtypeStruct((n, d), F32),
        input_output_aliases={3: 0},
        compiler_params=_cparams(("arbitrary",)),
        name="moe_combine",
    )(*plan, xs.reshape(n, d), route.reshape(n, LANES), loffv, mods_l, ys)
    return out.reshape(b, s, d)


def kernel(x, c, ctx, c_ctx, ada_w, ada_b, norm1_g, norm2_g, gdn_w_in, gdn_conv_w, gdn_a_log, gdn_dt_bias, gdn_norm_g, gdn_w_out, gmlp_w_in, gmlp_b_in, gmlp_ln_g, gmlp_ln_b, gmlp_w_s, gmlp_b_s, gmlp_w_out, router_w, router_b, moe_w_gu, moe_b_gu, moe_w_dn, moe_b_dn, final_g):
    b, t, d = x.shape
    n_ctx = ctx.shape[1]
    s = n_ctx + t
    depth = ada_w.shape[0]
    n_exp = router_w.shape[-1]
    assert t % TOK_TILE == 0 and n_ctx % TOK_TILE == 0 and d % LANES == 0 and n_exp <= LANES
    assert (b * s * TOP_K) % MOE_TILE == 0 and d == SUBLANES * LANES
    n_bt = s // TOK_TILE
    n_ct = n_ctx // TOK_TILE
    qk_w = GDN_HEADS * GDN_DK
    qkv_w = 3 * qk_w

    n_rows = -(-(b + 1) // 8) * 8
    cc = jnp.zeros((n_rows, d), F32).at[:b].set(c).at[b].set(c_ctx)
    mods = _ada_mods(cc, ada_w, ada_b).reshape(depth, n_rows, 1, N_ADA * d)

    def mod_map(bi, ti):
        return jnp.where(ti < n_ct, b, bi)

    mod_spec = pl.BlockSpec((None, 1, N_ADA * d), lambda bi, ti: (mod_map(bi, ti), 0, 0))
    tok_spec = lambda w: pl.BlockSpec((None, TOK_TILE, w), lambda bi, ti: (bi, ti, 0))
    full = lambda shape: pl.BlockSpec(shape, lambda bi, ti: (0,) * len(shape))
    tok_shape = lambda w, dt: jax.ShapeDtypeStruct((b, s, w), dt)

    xs = jnp.concatenate([ctx, x], axis=1)

    route_out_specs = [tok_spec(d), tok_spec(d), tok_spec(LANES),
                       pl.BlockSpec((None, 1, LANES), lambda bi, ti: (bi * n_bt + ti, 0, 0))]
    route_out_shapes = [tok_shape(d, F32), tok_shape(d, F32), tok_shape(LANES, F32),
                        jax.ShapeDtypeStruct((b * n_bt, 1, LANES), F32)]

    for i in range(depth):
        j = i // 2
        mods_l = mods[i]
        g1 = norm1_g[i].reshape(1, d)
        g2 = norm2_g[i].reshape(1, d)
        wr = jnp.zeros((d, LANES), F32).at[:, :n_exp].set(router_w[i])
        wr_hi = wr.astype(BF16)
        wr_lo = (wr - wr_hi.astype(F32)).astype(BF16)
        br = jnp.full((1, LANES), -1e30, F32).at[0, :n_exp].set(router_b[i])
        if i % 2 == 0:
            w_in = gdn_w_in[j]
            w_main = w_in[:, :qkv_w + qk_w].astype(BF16)
            n_ab = w_in.shape[1] - (qkv_w + qk_w)
            w_ab = jnp.zeros((d, LANES), F32).at[:, :n_ab].set(w_in[:, qkv_w + qk_w:]).astype(BF16)
            ab_c = jnp.zeros((8, LANES), F32)
            ab_c = ab_c.at[0, :2 * GDN_HEADS].set(-jnp.exp(gdn_a_log[j].reshape(-1)))
            ab_c = ab_c.at[1, :2 * GDN_HEADS].set(gdn_dt_bias[j].reshape(-1))
            qkv, z, gb = pl.pallas_call(
                functools.partial(_gdn_in_kernel, qkv_w=qkv_w),
                grid=(b, n_bt),
                in_specs=[tok_spec(d), mod_spec, full((1, d)), full((d, qkv_w + qk_w)), full((d, LANES)),
                          full((8, LANES))],
                out_specs=[tok_spec(qkv_w), tok_spec(qk_w), tok_spec(LANES)],
                out_shape=[tok_shape(qkv_w, BF16), tok_shape(qk_w, BF16), tok_shape(LANES, F32)],
                compiler_params=_cparams(("parallel", "parallel")),
                name="gdn_in",
            )(xs, mods_l, g1, w_main, w_ab, ab_c)

            qkv_c = pl.pallas_call(
                functools.partial(_gdn_conv_kernel, n_ctx=n_ctx, n_lat=t, qk_blocks=2 * GDN_HEADS),
                grid=(b, qkv_w // LANES),
                in_specs=[pl.BlockSpec((None, s, LANES), lambda bi, ci: (bi, 0, ci)),
                          pl.BlockSpec((GDN_CONV, LANES), lambda bi, ci: (0, ci))],
                out_specs=pl.BlockSpec((None, s, LANES), lambda bi, ci: (bi, 0, ci)),
                out_shape=tok_shape(qkv_w, BF16),
                scratch_shapes=[pltpu.VMEM((s + 24, LANES), F32)],
                compiler_params=_cparams(("parallel", "parallel")),
                name="gdn_conv",
            )(qkv, gdn_conv_w[j])

            gbt = jnp.swapaxes(gb[:, :, :4 * GDN_HEADS], 1, 2)

            def bwd_tile(ti):
                return jnp.where(ti < n_ct, n_ct - 1 - ti, n_bt - 1 - (ti - n_ct))

            hw = SCAN_HEADS * GDN_DK
            n_hb = GDN_HEADS // SCAN_HEADS

            def hd_spec(off, tmap):
                return pl.BlockSpec((None, TOK_TILE, hw), lambda bi, hi, ti: (bi, tmap(ti), off + hi))

            fwd_tile = lambda ti: ti
            scan_in = []
            for tmap in (fwd_tile, bwd_tile):
                scan_in += [hd_spec(0, tmap), hd_spec(n_hb, tmap), hd_spec(2 * n_hb, tmap),
                            pl.BlockSpec((None, TOK_TILE, LANES), lambda bi, hi, ti, tmap=tmap: (bi, tmap(ti), 0)),
                            pl.BlockSpec((None, 4 * GDN_HEADS, TOK_TILE),
                                         lambda bi, hi, ti, tmap=tmap: (bi, 0, tmap(ti)))]
            o_f, o_b = pl.pallas_call(
                functools.partial(_gdn_scan_kernel, scale=GDN_DK ** -0.5),
                grid=(b, n_hb, n_bt),
                in_specs=scan_in,
                out_specs=[hd_spec(0, fwd_tile), hd_spec(0, bwd_tile)],
                out_shape=[tok_shape(qk_w, BF16), tok_shape(qk_w, BF16)],
                scratch_shapes=[pltpu.VMEM((SCAN_HEADS, GDN_DK, GDN_DK), F32),
                                pltpu.VMEM((SCAN_HEADS, GDN_DK, GDN_DK), F32)],
                compiler_params=_cparams(("parallel", "parallel", "arbitrary")),
                name="gdn_scan",
            )(qkv_c, qkv_c, qkv_c, gb, gbt, qkv_c, qkv_c, qkv_c, gb, gbt)

            xs, h2, route, counts = pl.pallas_call(
                _gdn_out_kernel,
                grid=(b, n_bt),
                in_specs=[tok_spec(qk_w), tok_spec(qk_w), tok_spec(qk_w), tok_spec(d), mod_spec,
                          full((1, LANES)), full((qk_w, d)), full((1, d)), full((d, LANES)), full((d, LANES)),
                          full((1, LANES))],
                out_specs=route_out_specs,
                out_shape=route_out_shapes,
                input_output_aliases={3: 0},
                compiler_params=_cparams(("parallel", "parallel")),
                name="gdn_out",
            )(o_f, o_b, z, xs, mods_l, gdn_norm_g[j].reshape(1, LANES), gdn_w_out[j].astype(BF16), g2,
              wr_hi, wr_lo, br)
        else:
            width = gmlp_w_out.shape[1]
            xs, h2, route, counts = pl.pallas_call(
                _gmlp_kernel,
                grid=(b, n_bt),
                in_specs=[tok_spec(d), mod_spec, full((1, d)), full((d, 2 * width)), full((1, 2 * width)),
                          full((1, width)), full((1, width)), full((GMLP_GROUPS, GMLP_CHUNK, GMLP_CHUNK)),
                          full((GMLP_CHUNK, GMLP_GROUPS)), full((width, d)), full((1, d)), full((d, LANES)),
                          full((d, LANES)), full((1, LANES))],
                out_specs=route_out_specs,
                out_shape=route_out_shapes,
                input_output_aliases={0: 0},
                compiler_params=_cparams(("parallel", "parallel")),
                name="gmlp",
            )(xs, mods_l, g1, gmlp_w_in[j].astype(BF16), gmlp_b_in[j].reshape(1, -1),
              gmlp_ln_g[j].reshape(1, -1), gmlp_ln_b[j].reshape(1, -1), gmlp_w_s[j].astype(BF16),
              gmlp_b_s[j].T, gmlp_w_out[j].astype(BF16), g2, wr_hi, wr_lo, br)

        xs = _moe(xs, h2, route, counts, mods_l, mod_map, i, moe_w_gu, moe_b_gu, moe_w_dn, moe_b_dn, n_bt)

    return pl.pallas_call(
        _final_kernel,
        grid=(b, t // TOK_TILE),
        in_specs=[pl.BlockSpec((None, TOK_TILE, d), lambda bi, ti: (bi, n_ct + ti, 0)), full((1, d))],
        out_specs=pl.BlockSpec((None, TOK_TILE, d), lambda bi, ti: (bi, ti, 0)),
        out_shape=jax.ShapeDtypeStruct((b, t, d), x.dtype),
        compiler_params=_cparams(("parallel", "parallel")),
        name="final_norm",
    )(xs, final_g.reshape(1, d))
```

```python
import functools

import jax
import jax.numpy as jnp
from jax import lax
from jax.experimental import pallas as pl
from jax.experimental.pallas import tpu as pltpu

F32 = jnp.float32
BF16 = jnp.bfloat16

NORM_EPS = 1e-6
N_ADA = 6
GDN_HEADS = 8
GDN_DK = 128
GDN_CONV = 5
GDN_CHUNK = 64
GMLP_CHUNK = 128
GMLP_GROUPS = 8
TOP_K = 4
SWIGLU_ALPHA = 1.702
SWIGLU_LIMIT = 7.0

LANES = 128
SUBLANES = 8
TOK_TILE = 256
MOE_TILE = 512
INV_BLOCK = 16
SCAN_HEADS = 8
VMEM_LIMIT = 56 * 1024 * 1024


def _cparams(sem):
    return pltpu.CompilerParams(dimension_semantics=sem, vmem_limit_bytes=VMEM_LIMIT)


def _dot(a, b):
    return jnp.dot(a.astype(BF16), b.astype(BF16), preferred_element_type=F32)


def _dot_nt(a, b):
    return lax.dot_general(a.astype(BF16), b.astype(BF16), (((1,), (1,)), ((), ())),
                           preferred_element_type=F32)


def _dot_tn(a, b):
    return lax.dot_general(a.astype(BF16), b.astype(BF16), (((0,), (0,)), ((), ())),
                           preferred_element_type=F32)


def _split(a):
    hi = a.astype(BF16)
    lo = (a - hi.astype(F32)).astype(BF16)
    return hi, lo


def _dot3(a, b):
    ah, al = _split(a)
    bh, bl = _split(b)
    d = lambda x, y: jnp.dot(x, y, preferred_element_type=F32)
    return d(ah, bh) + (d(ah, bl) + d(al, bh))


def _rms_scale(xf):
    return xf * lax.rsqrt(jnp.mean(xf * xf, axis=-1, keepdims=True) + NORM_EPS)


def _silu(x):
    return x * jax.nn.sigmoid(x)


def _mod_part(m, k, d):
    return m[:, k * d:(k + 1) * d]


def _ada_kernel(cc_ref, w_ref, b_ref, o_ref):
    s = _silu(cc_ref[...])
    o_ref[...] = _dot3(s, w_ref[...]) + b_ref[...]


def _ada_mods(cc, ada_w, ada_b):
    depth, d, n = ada_w.shape
    r = cc.shape[0]
    tn = n // 4
    return pl.pallas_call(
        _ada_kernel,
        grid=(depth, n // tn),
        in_specs=[pl.BlockSpec((r, d), lambda l, j: (0, 0)),
                  pl.BlockSpec((None, d, tn), lambda l, j: (l, 0, j)),
                  pl.BlockSpec((None, 1, tn), lambda l, j: (l, 0, j))],
        out_specs=pl.BlockSpec((None, r, tn), lambda l, j: (l, 0, j)),
        out_shape=jax.ShapeDtypeStruct((depth, r, n), F32),
        compiler_params=_cparams(("parallel", "parallel")),
        name="ada_mods",
    )(cc, ada_w, ada_b.reshape(depth, 1, n))


def _route_epilogue(x_new, m, g2, wr_hi, wr_lo, br, tri, h2_ref, route_ref, cnt_ref):
    d = x_new.shape[-1]
    tm = x_new.shape[0]
    h2 = (_rms_scale(x_new) * g2) * (1.0 + _mod_part(m, 4, d)) + _mod_part(m, 3, d)
    h2_ref[...] = h2
    hh, hl = _split(h2)
    dd = lambda a, b: jnp.dot(a, b, preferred_element_type=F32)
    logits = dd(hh, wr_hi) + (dd(hh, wr_lo) + dd(hl, wr_hi)) + br
    lane = lax.broadcasted_iota(jnp.int32, (tm, LANES), 1).astype(F32)
    vals, idxs = [], []
    l = logits
    for _ in range(TOP_K):
        mx = jnp.max(l, axis=1, keepdims=True)
        ix = jnp.min(jnp.where(l == mx, lane, float(LANES)), axis=1, keepdims=True)
        vals.append(mx)
        idxs.append(ix)
        l = jnp.where(lane == ix, -jnp.inf, l)
    es = [jnp.exp(v - vals[0]) for v in vals]
    den = es[0] + es[1] + es[2] + es[3]
    hots = [(lane == ix).astype(F32) for ix in idxs]
    oh4 = hots[0] + hots[1] + hots[2] + hots[3]
    prefix = jnp.dot(tri, oh4.astype(BF16), preferred_element_type=F32)
    route = jnp.zeros((tm, LANES), F32)
    for k in range(TOP_K):
        rank = jnp.sum(hots[k] * prefix, axis=1, keepdims=True)
        route = jnp.where(lane == float(k), idxs[k], route)
        route = jnp.where(lane == float(TOP_K + k), es[k] / den, route)
        route = jnp.where(lane == float(2 * TOP_K + k), rank, route)
    route_ref[...] = route
    cnt_ref[...] = jnp.sum(oh4, axis=0, keepdims=True)


def _strict_lower(n):
    i = lax.broadcasted_iota(jnp.int32, (n, n), 0)
    j = lax.broadcasted_iota(jnp.int32, (n, n), 1)
    return (j < i).astype(BF16)


def _gdn_in_kernel(x_ref, m_ref, g1_ref, w_ref, wab_ref, ab_ref, qkv_ref, z_ref, gb_ref, *, qkv_w):
    d = x_ref.shape[-1]
    m = m_ref[...]
    h = (_rms_scale(x_ref[...]) * g1_ref[...]) * (1.0 + _mod_part(m, 1, d)) + _mod_part(m, 0, d)
    hb = h.astype(BF16)
    p = jnp.dot(hb, w_ref[...], preferred_element_type=F32)
    qkv_ref[...] = p[:, :qkv_w].astype(BF16)
    z_ref[...] = p[:, qkv_w:].astype(BF16)
    ab = jnp.dot(hb, wab_ref[...], preferred_element_type=F32)
    neg_a = ab_ref[0:1, :]
    dt_b = ab_ref[1:2, :]
    xs = ab + dt_b
    softplus = jnp.maximum(xs, 0.0) + jnp.log(1.0 + jnp.exp(-jnp.abs(xs)))
    lane = lax.broadcasted_iota(jnp.int32, ab.shape, 1)
    gb_ref[...] = jnp.where(lane < 2 * GDN_HEADS, neg_a * softplus, jax.nn.sigmoid(ab))


def _gdn_conv_kernel(x_ref, w_ref, o_ref, pad_ref, *, n_ctx, n_lat, qk_blocks):
    tile = TOK_TILE
    zeros8 = jnp.zeros((8, LANES), F32)
    pad_ref[0:8, :] = zeros8
    pad_ref[8 + n_ctx:16 + n_ctx, :] = zeros8
    pad_ref[16 + n_ctx + n_lat:24 + n_ctx + n_lat, :] = zeros8
    for r in range(0, n_ctx + n_lat, tile):
        base = 8 + r if r < n_ctx else 16 + r
        pad_ref[base:base + tile, :] = x_ref[r:r + tile, :].astype(F32)
    w = w_ref[...]
    is_qk = pl.program_id(1) < qk_blocks
    half = GDN_CONV // 2
    for r in range(0, n_ctx + n_lat, tile):
        base = 8 + r if r < n_ctx else 16 + r
        acc = jnp.zeros((tile, LANES), F32)
        for k in range(GDN_CONV):
            s = base + k - half
            acc = acc + pad_ref[s:s + tile, :] * w[k:k + 1, :]
        y = _silu(acc)
        yn = y * lax.rsqrt(jnp.sum(y * y, axis=-1, keepdims=True) + NORM_EPS)
        o_ref[r:r + tile, :] = jnp.where(is_qk, yn, y).astype(BF16)


def _each(fn, *seqs):
    return [fn(*args) for args in zip(*seqs)]


def _unit_tri_inverse(lmats, bi, bj):
    n = lmats[0].shape[0]
    eye = (lax.broadcasted_iota(jnp.int32, (n, n), 0) == lax.broadcasted_iota(jnp.int32, (n, n), 1)).astype(F32)
    ld = _each(lambda l: jnp.where(bi == bj, l, 0.0), lmats)
    x = _each(lambda l: eye - l, ld)
    p = _each(_dot, ld, ld)
    width = 2
    while True:
        x = _each(lambda xi, pi: xi + _dot(xi, pi), x, p)
        width *= 2
        if width >= INV_BLOCK:
            break
        p = _each(_dot, p, p)
    span = 2
    while INV_BLOCK * span <= GDN_CHUNK:
        half = span // 2
        off = ((bi // span) == (bj // span)) & ((bi // half) != (bj // half))
        xc = _each(lambda xi, l: _dot(xi, jnp.where(off, l, 0.0)), x, lmats)
        x = _each(lambda xi, xci: xi - _dot(xci, xi), x, xc)
        span *= 2
    return x


def _gdn_scan_kernel(qf_ref, kf_ref, vf_ref, gbf_ref, gtf_ref, qb_ref, kb_ref, vb_ref, gbb_ref, gtb_ref,
                     of_ref, ob_ref, sf_ref, sb_ref, *, scale):
    h0 = pl.program_id(1) * (qf_ref.shape[1] // GDN_DK)

    @pl.when(pl.program_id(2) == 0)
    def _():
        sf_ref[...] = jnp.zeros_like(sf_ref)
        sb_ref[...] = jnp.zeros_like(sb_ref)

    tm = qf_ref.shape[0]
    dk = GDN_DK
    hb = qf_ref.shape[1] // dk
    cs = GDN_CHUNK
    n_chunks = tm // cs
    lane = lax.broadcasted_iota(jnp.int32, (tm, LANES), 1)
    ii = lax.broadcasted_iota(jnp.int32, (tm, tm), 0)
    jj = lax.broadcasted_iota(jnp.int32, (tm, tm), 1)
    same = (ii // cs) == (jj // cs)
    bi, bj = ii // INV_BLOCK, jj // INV_BLOCK

    def col(gb, idx):
        return jnp.sum(jnp.where(lane == idx, gb, 0.0), axis=1, keepdims=True)

    probs = [(hh, d) for hh in range(hb) for d in range(2)]
    tri_incl = [same & (ii >= jj), same & (ii <= jj)]
    tri_strict = [same & (ii > jj), same & (ii < jj)]
    incl = [tri_incl[d] for _, d in probs]
    incl_t = [tri_incl[1 - d] for _, d in probs]
    strict = [tri_strict[d] for _, d in probs]
    gbs = [gbf_ref[...], gbb_ref[...]]
    gts = [gtf_ref, gtb_ref]
    qkv_refs = [(qf_ref, kf_ref, vf_ref), (qb_ref, kb_ref, vb_ref)]
    o_refs = [of_ref, ob_ref]
    s_refs = [sf_ref, sb_ref]
    g_col = [col(gbs[d], d * GDN_HEADS + h0 + hh) for hh, d in probs]
    beta = [col(gbs[d], (2 + d) * GDN_HEADS + h0 + hh) for hh, d in probs]
    g_row = [gts[d][pl.ds(d * GDN_HEADS + h0 + hh, 1), :] for hh, d in probs]
    q, k, v = [[qkv_refs[d][j][:, hh * dk:(hh + 1) * dk].astype(F32) for hh, d in probs] for j in range(3)]

    gc_col = _each(lambda m, g: jnp.sum(jnp.where(m, g, 0.0), axis=1, keepdims=True), incl, g_row)
    gc_row = _each(lambda m, g: jnp.sum(jnp.where(m, g, 0.0), axis=0, keepdims=True), incl_t, g_col)
    g_tot = _each(lambda g: jnp.sum(jnp.where(same, g, 0.0), axis=1, keepdims=True), g_row)
    decay = _each(lambda m, c, r: jnp.exp(jnp.where(m, c - r, -jnp.inf)), incl, gc_col, gc_row)
    qkk = _each(lambda qi, ki: _dot_nt(jnp.concatenate([qi, ki], axis=0), ki), q, k)
    lmat = _each(lambda m, b, x, dc: jnp.where(m, b * x[tm:] * dc, 0.0), strict, beta, qkk, decay)
    t_inv = _unit_tri_inverse(lmat, bi, bj)
    eg = _each(jnp.exp, gc_col)
    wu = _each(lambda t, ki, vi, b, e: _dot(t, jnp.concatenate([ki * (b * e), vi * b], axis=1)),
               t_inv, k, v, beta, eg)
    awu = _each(lambda x, dc, w: _dot(x[:tm] * (scale * dc), w), qkk, decay, wu)
    q_eff = _each(lambda qi, e, a: qi * (scale * e) - a[:, :dk], q, eg, awu)
    k_tail = _each(lambda ki, t, c: ki * jnp.exp(t - c), k, g_tot, gc_col)
    s = [s_refs[d][hh] for hh, d in probs]
    for step in range(n_chunks):
        for p, (hh, d) in enumerate(probs):
            c = step if d == 0 else n_chunks - 1 - step
            sl = slice(c * cs, (c + 1) * cs)
            kwu = _dot_tn(k_tail[p][sl], wu[p][sl])
            o_refs[d][sl, hh * dk:(hh + 1) * dk] = (
                _dot(q_eff[p][sl], s[p]) + awu[p][sl, dk:]).astype(o_refs[d].dtype)
            s[p] = s[p] * jnp.exp(g_tot[p][c * cs:c * cs + 1, :]) + (kwu[:, dk:] - _dot(kwu[:, :dk], s[p]))
    for p, (hh, d) in enumerate(probs):
        s_refs[d][hh] = s[p]


def _gdn_out_kernel(of_ref, ob_ref, z_ref, x_ref, m_ref, ng_ref, w_ref, g2_ref, wrh_ref, wrl_ref, br_ref,
                    xo_ref, h2_ref, route_ref, cnt_ref):
    d = x_ref.shape[-1]
    m = m_ref[...]
    o = of_ref[...].astype(F32) + ob_ref[...].astype(F32)
    z = z_ref[...].astype(F32)
    ng = ng_ref[...]
    parts = []
    for hd in range(o.shape[1] // LANES):
        sl = slice(hd * LANES, (hd + 1) * LANES)
        parts.append(_rms_scale(o[:, sl]) * ng * _silu(z[:, sl]))
    y = jnp.dot(jnp.concatenate(parts, axis=1).astype(BF16), w_ref[...], preferred_element_type=F32)
    x_new = x_ref[...] + _mod_part(m, 2, d) * y
    xo_ref[...] = x_new
    _route_epilogue(x_new, m, g2_ref[...], wrh_ref[...], wrl_ref[...], br_ref[...],
                    _strict_lower(x_new.shape[0]), h2_ref, route_ref, cnt_ref)


def _gmlp_kernel(x_ref, m_ref, g1_ref, win_ref, bin_ref, lng_ref, lnb_ref, ws_ref, bs_ref, wout_ref,
                 g2_ref, wrh_ref, wrl_ref, br_ref, xo_ref, h2_ref, route_ref, cnt_ref):
    d = x_ref.shape[-1]
    tm = x_ref.shape[0]
    m = m_ref[...]
    x = x_ref[...]
    width = wout_ref.shape[0]
    gw = width // GMLP_GROUPS
    gelu = lambda a: 0.5 * a * (1.0 + lax.erf(a * (2.0 ** -0.5)))
    xc = [x[c * GMLP_CHUNK:(c + 1) * GMLP_CHUNK] for c in range(tm // GMLP_CHUNK)]
    hb = _each(lambda a: ((_rms_scale(a) * g1_ref[...]) * (1.0 + _mod_part(m, 1, d))
                          + _mod_part(m, 0, d)).astype(BF16), xc)
    v = _each(lambda a: gelu(jnp.dot(a, win_ref[:, width:], preferred_element_type=F32) + bin_ref[:, width:]), hb)
    vc = _each(lambda a: a - jnp.mean(a, axis=-1, keepdims=True), v)
    vb = _each(lambda a: ((a * lax.rsqrt(jnp.mean(a * a, axis=-1, keepdims=True) + NORM_EPS)) * lng_ref[...]
                          + lnb_ref[...]).astype(BF16), vc)
    s = _each(lambda a: jnp.concatenate(
        [jnp.dot(ws_ref[g], a[:, g * gw:(g + 1) * gw], preferred_element_type=F32) + bs_ref[:, g:g + 1]
         for g in range(GMLP_GROUPS)], axis=1), vb)
    u = _each(lambda a: gelu(jnp.dot(a, win_ref[:, :width], preferred_element_type=F32) + bin_ref[:, :width]), hb)
    y = _each(lambda a, b: jnp.dot((a * b).astype(BF16), wout_ref[...], preferred_element_type=F32), u, s)
    x_new = x + _mod_part(m, 2, d) * jnp.concatenate(y, axis=0)
    xo_ref[...] = x_new
    _route_epilogue(x_new, m, g2_ref[...], wrh_ref[...], wrl_ref[...], br_ref[...],
                    _strict_lower(tm), h2_ref, route_ref, cnt_ref)


def _local_positions(route, loffv):
    tm = route.shape[0]
    lane = lax.broadcasted_iota(jnp.int32, (tm, LANES), 1).astype(F32)
    out = []
    for k in range(TOP_K):
        first = jnp.sum(jnp.where(lane == route[:, k:k + 1], loffv, 0.0), axis=1, keepdims=True)
        out.append(first + route[:, 2 * TOP_K + k:2 * TOP_K + k + 1])
    return out


def _rows(first, count):
    return pl.ds(pl.multiple_of(first * SUBLANES, SUBLANES), count * SUBLANES)


def _lane_block(first, count, sb):
    return pl.ds(first * SUBLANES + sb, count, stride=SUBLANES)


def _run_pieces(cnt, fn):
    size = TOK_TILE
    while size >= 1:
        def piece(size=size):
            fn(cnt & ~(2 * size - 1), size)
        pl.when((cnt & size) != 0)(piece)
        size //= 2


def _dispatch_kernel(cnt_s, loff_s, base_s, pstart_s, plen_s, h_ref, route_ref, loffv_ref, o_ref, buf_ref, sem):
    i = pl.program_id(0)
    n_exp = plen_s.shape[0]
    tm = h_ref.shape[0]
    n_pairs = tm * TOP_K
    slot = i % 2
    buf = buf_ref.at[slot]
    lpos = _local_positions(route_ref[...], loffv_ref[...])
    ii = lax.broadcasted_iota(jnp.int32, (tm, tm), 0)
    jj = lax.broadcasted_iota(jnp.int32, (tm, tm), 1)
    lpos_rows = [jnp.sum(jnp.where(ii == jj, lp, 0.0), axis=0, keepdims=True) for lp in lpos]
    hb = h_ref[...].astype(BF16)
    for blk in range(TOP_K):
        pair = (ii + blk * tm).astype(F32)
        perm = jnp.zeros((tm, tm), F32)
        for k in range(TOP_K):
            perm = perm + (pair == lpos_rows[k]).astype(F32)
        hp = jnp.dot(perm.astype(BF16), hb, preferred_element_type=F32)
        for sb in range(hp.shape[1] // LANES):
            buf[_lane_block(blk * tm, tm, sb), :] = hp[:, sb * LANES:(sb + 1) * LANES]

    def start_run(e, carry):
        lo, bs = loff_s[i * n_exp + e], base_s[i * n_exp + e]
        _run_pieces(cnt_s[i * n_exp + e], lambda off, size: pltpu.make_async_copy(
            buf.at[_rows(lo + off, size)], o_ref.at[_rows(bs + off, size)], sem.at[slot]).start())
        return carry

    lax.fori_loop(0, n_exp, start_run, 0)

    def wait_tile(sl):
        pltpu.make_async_copy(buf_ref.at[sl], o_ref.at[_rows(0, n_pairs)], sem.at[sl]).wait()

    pl.when(i > 0)(lambda: wait_tile(1 - slot))

    @pl.when(i == pl.num_programs(0) - 1)
    def _():
        wait_tile(slot)
        buf[0:MOE_TILE * SUBLANES, :] = jnp.zeros((MOE_TILE * SUBLANES, LANES), F32)

        def pad_copy(e, off, size):
            return pltpu.make_async_copy(buf.at[_rows(0, size)], o_ref.at[_rows(pstart_s[e] + off, size)],
                                         sem.at[slot])

        def fill(e, carry):
            _run_pieces(plen_s[e], lambda off, size: pad_copy(e, off, size).start())
            _run_pieces(plen_s[e], lambda off, size: pad_copy(e, off, size).wait())
            return carry

        lax.fori_loop(0, n_exp, fill, 0)


def _expert_kernel(te_ref, nt_ref, x_ref, wgu_ref, bgu_ref, wdn_ref, bdn_ref, o_ref, xs_ref, wgu_bf, wdn_bf):
    j = pl.program_id(0)
    n_rows = xs_ref.shape[0]

    @pl.when((j == 0) | (te_ref[j] != te_ref[jnp.maximum(j - 1, 0)]))
    def _():
        wgu_bf[...] = wgu_ref[...].astype(BF16)
        wdn_bf[...] = wdn_ref[...].astype(BF16)

    @pl.when(j < nt_ref[0])
    def _():
        de = wdn_ref.shape[0]
        for sb in range(SUBLANES):
            xs_ref[:, sb * LANES:(sb + 1) * LANES] = x_ref[_lane_block(0, n_rows, sb), :].astype(BF16)
        gu = jnp.dot(xs_ref[...], wgu_bf[...], preferred_element_type=F32) + bgu_ref[...]
        gate = jnp.minimum(gu[:, :de], SWIGLU_LIMIT)
        up = jnp.clip(gu[:, de:], -SWIGLU_LIMIT, SWIGLU_LIMIT)
        glu = gate * jax.nn.sigmoid(gate * SWIGLU_ALPHA)
        act = ((up + 1.0) * glu).astype(BF16)
        y = jnp.dot(act, wdn_bf[...], preferred_element_type=F32) + bdn_ref[...]
        for sb in range(SUBLANES):
            o_ref[_lane_block(0, n_rows, sb), :] = y[:, sb * LANES:(sb + 1) * LANES]

    @pl.when(pl.program_id(0) >= nt_ref[0])
    def _():
        o_ref[...] = jnp.zeros_like(o_ref)


def _combine_kernel(cnt_s, loff_s, base_s, x_ref, route_ref, loffv_ref, m_ref, y_ref, o_ref, buf_ref, sem):
    i = pl.program_id(0)
    n_exp = cnt_s.shape[0] // pl.num_programs(0)
    tm, d = x_ref.shape
    n_pairs = tm * TOP_K
    slot = i % 2

    def fetch(tile, sl):
        def start_run(e, carry):
            lo, bs = loff_s[tile * n_exp + e], base_s[tile * n_exp + e]
            _run_pieces(cnt_s[tile * n_exp + e], lambda off, size: pltpu.make_async_copy(
                y_ref.at[_rows(bs + off, size)], buf_ref.at[sl].at[_rows(lo + off, size)], sem.at[sl]).start())
            return carry

        lax.fori_loop(0, n_exp, start_run, 0)

    pl.when(i == 0)(lambda: fetch(i, slot))
    pl.when(i + 1 < pl.num_programs(0))(lambda: fetch(i + 1, 1 - slot))
    buf = buf_ref.at[slot]
    route = route_ref[...]
    lpos = _local_positions(route, loffv_ref[...])
    jj = lax.broadcasted_iota(jnp.int32, (tm, tm), 1)
    combs = []
    for blk in range(TOP_K):
        pair = (jj + blk * tm).astype(F32)
        comb = jnp.zeros((tm, tm), F32)
        for k in range(TOP_K):
            comb = comb + jnp.where(pair == lpos[k], route[:, TOP_K + k:TOP_K + k + 1], 0.0)
        combs.append(comb.astype(BF16))
    pltpu.make_async_copy(y_ref.at[_rows(0, n_pairs)], buf, sem.at[slot]).wait()
    acc = jnp.zeros((tm, d), F32)
    for blk in range(TOP_K):
        y = jnp.concatenate([buf[_lane_block(blk * tm, tm, sb), :].astype(BF16) for sb in range(SUBLANES)],
                            axis=1)
        acc = acc + jnp.dot(combs[blk], y, preferred_element_type=F32)
    o_ref[...] = x_ref[...] + _mod_part(m_ref[...], 5, d) * acc


def _final_kernel(x_ref, g_ref, o_ref):
    o_ref[...] = _rms_scale(x_ref[...]) * g_ref[...]


def _moe(xs, h2, route, counts, mods_l, mod_map, layer, w_gu, b_gu, w_dn, b_dn, n_bt, t0):
    b, s, d = xs.shape
    n = b * s
    n_exp = w_gu.shape[1]
    n_t = n_bt - t0
    nt_tok = b * n_t
    assert (nt_tok * TOK_TILE * TOP_K) % MOE_TILE == 0
    max_tiles = (nt_tok * TOK_TILE * TOP_K) // MOE_TILE + n_exp
    n_pairs = TOK_TILE * TOP_K
    tok_tile = lambda i: (i // n_t) * n_bt + t0 + i % n_t
    cnt = counts.reshape(nt_tok, LANES)[:, :n_exp].astype(jnp.int32)
    totals = jnp.sum(cnt, axis=0)
    tiles_e = (totals + MOE_TILE - 1) // MOE_TILE
    tile_end = jnp.cumsum(tiles_e)
    starts = (tile_end - tiles_e) * MOE_TILE
    base = starts[None, :] + jnp.cumsum(cnt, axis=0) - cnt
    loff = jnp.cumsum(cnt, axis=1) - cnt
    loffv = jnp.zeros((nt_tok, 1, LANES), F32).at[:, 0, :n_exp].set(loff.astype(F32))
    tile_ids = jnp.arange(max_tiles, dtype=jnp.int32)
    tile_expert = jnp.minimum(jnp.sum((tile_end[None, :] <= tile_ids[:, None]).astype(jnp.int32), axis=1),
                              n_exp - 1)
    n_tiles = tile_end[-1:].astype(jnp.int32)
    plan = (cnt.reshape(-1), loff.reshape(-1), base.reshape(-1))
    sorted_shape = jax.ShapeDtypeStruct((max_tiles * MOE_TILE * SUBLANES, LANES), F32)

    hs = pl.pallas_call(
        _dispatch_kernel,
        grid_spec=pltpu.PrefetchScalarGridSpec(
            num_scalar_prefetch=5,
            grid=(nt_tok,),
            in_specs=[pl.BlockSpec((TOK_TILE, d), lambda i, *_: (tok_tile(i), 0)),
                      pl.BlockSpec((TOK_TILE, LANES), lambda i, *_: (tok_tile(i), 0)),
                      pl.BlockSpec((None, 1, LANES), lambda i, *_: (i, 0, 0))],
            out_specs=pl.BlockSpec(memory_space=pl.ANY),
            scratch_shapes=[pltpu.VMEM((2, n_pairs * SUBLANES, LANES), F32), pltpu.SemaphoreType.DMA((2,))]),
        out_shape=sorted_shape,
        compiler_params=_cparams(("arbitrary",)),
        name="moe_dispatch",
    )(*plan, starts + totals, tiles_e * MOE_TILE - totals, h2.reshape(n, d), route.reshape(n, LANES), loffv)

    de = w_dn.shape[2]
    ys = pl.pallas_call(
        _expert_kernel,
        grid_spec=pltpu.PrefetchScalarGridSpec(
            num_scalar_prefetch=2,
            grid=(max_tiles,),
            in_specs=[pl.BlockSpec((MOE_TILE * SUBLANES, LANES), lambda j, te, nt: (jnp.where(j < nt[0], j, 0), 0)),
                      pl.BlockSpec((None, None, d, 2 * de), lambda j, te, nt: (layer, te[j], 0, 0)),
                      pl.BlockSpec((None, None, 1, 2 * de), lambda j, te, nt: (layer, te[j], 0, 0)),
                      pl.BlockSpec((None, None, de, d), lambda j, te, nt: (layer, te[j], 0, 0)),
                      pl.BlockSpec((None, None, 1, d), lambda j, te, nt: (layer, te[j], 0, 0))],
            out_specs=pl.BlockSpec((MOE_TILE * SUBLANES, LANES), lambda j, te, nt: (j, 0)),
            scratch_shapes=[pltpu.VMEM((MOE_TILE, d), BF16), pltpu.VMEM((d, 2 * de), BF16),
                            pltpu.VMEM((de, d), BF16)]),
        out_shape=sorted_shape,
        compiler_params=_cparams(("arbitrary",)),
        name="moe_experts",
    )(tile_expert, n_tiles, hs, w_gu, b_gu.reshape(-1, n_exp, 1, 2 * de), w_dn, b_dn.reshape(-1, n_exp, 1, d))

    out = pl.pallas_call(
        _combine_kernel,
        grid_spec=pltpu.PrefetchScalarGridSpec(
            num_scalar_prefetch=3,
            grid=(nt_tok,),
            in_specs=[pl.BlockSpec((TOK_TILE, d), lambda i, *_: (tok_tile(i), 0)),
                      pl.BlockSpec((TOK_TILE, LANES), lambda i, *_: (tok_tile(i), 0)),
                      pl.BlockSpec((None, 1, LANES), lambda i, *_: (i, 0, 0)),
                      pl.BlockSpec((None, 1, N_ADA * d), lambda i, *_: (mod_map(i // n_t, t0 + i % n_t), 0, 0)),
                      pl.BlockSpec(memory_space=pl.ANY)],
            out_specs=pl.BlockSpec((TOK_TILE, d), lambda i, *_: (tok_tile(i), 0)),
            scratch_shapes=[pltpu.VMEM((2, n_pairs * SUBLANES, LANES), F32), pltpu.SemaphoreType.DMA((2,))]),
        out_shape=jax.ShapeDtypeStruct((n, d), F32),
        input_output_aliases={3: 0},
        compiler_params=_cparams(("arbitrary",)),
        name="moe_combine",
    )(*plan, xs.reshape(n, d), route.reshape(n, LANES), loffv, mods_l, ys)
    return out.reshape(b, s, d)


def kernel(x, c, ctx, c_ctx, ada_w, ada_b, norm1_g, norm2_g, gdn_w_in, gdn_conv_w, gdn_a_log, gdn_dt_bias, gdn_norm_g, gdn_w_out, gmlp_w_in, gmlp_b_in, gmlp_ln_g, gmlp_ln_b, gmlp_w_s, gmlp_b_s, gmlp_w_out, router_w, router_b, moe_w_gu, moe_b_gu, moe_w_dn, moe_b_dn, final_g):
    b, t, d = x.shape
    n_ctx = ctx.shape[1]
    s = n_ctx + t
    depth = ada_w.shape[0]
    n_exp = router_w.shape[-1]
    assert t % TOK_TILE == 0 and n_ctx % TOK_TILE == 0 and d % LANES == 0 and n_exp <= LANES
    assert (b * s * TOP_K) % MOE_TILE == 0 and d == SUBLANES * LANES
    n_bt = s // TOK_TILE
    n_ct = n_ctx // TOK_TILE
    qk_w = GDN_HEADS * GDN_DK
    qkv_w = 3 * qk_w

    n_rows = -(-(b + 1) // 8) * 8
    cc = jnp.zeros((n_rows, d), F32).at[:b].set(c).at[b].set(c_ctx)
    mods = _ada_mods(cc, ada_w, ada_b).reshape(depth, n_rows, 1, N_ADA * d)

    def mod_map(bi, ti):
        return jnp.where(ti < n_ct, b, bi)

    mod_spec_from = lambda t0: pl.BlockSpec((None, 1, N_ADA * d), lambda bi, ti: (mod_map(bi, ti + t0), 0, 0))
    tok_spec_from = lambda t0: (lambda w: pl.BlockSpec((None, TOK_TILE, w), lambda bi, ti: (bi, ti + t0, 0)))
    mod_spec, tok_spec = mod_spec_from(0), tok_spec_from(0)
    full = lambda shape: pl.BlockSpec(shape, lambda bi, ti: (0,) * len(shape))
    tok_shape = lambda w, dt: jax.ShapeDtypeStruct((b, s, w), dt)

    xs = jnp.concatenate([ctx, x], axis=1)
    last_ctx_reader = ((depth - 1) // 2) * 2

    for i in range(depth):
        j = i // 2
        t0 = 0 if i < last_ctx_reader else n_ct
        n_t = n_bt - t0
        mod_spec_o, tok_spec_o = mod_spec_from(t0), tok_spec_from(t0)
        route_out_specs = [tok_spec_o(d), tok_spec_o(d), tok_spec_o(LANES),
                           pl.BlockSpec((None, 1, LANES), lambda bi, ti, n_t=n_t: (bi * n_t + ti, 0, 0))]
        route_out_shapes = [tok_shape(d, F32), tok_shape(d, F32), tok_shape(LANES, F32),
                            jax.ShapeDtypeStruct((b * n_t, 1, LANES), F32)]
        mods_l = mods[i]
        g1 = norm1_g[i].reshape(1, d)
        g2 = norm2_g[i].reshape(1, d)
        wr = jnp.zeros((d, LANES), F32).at[:, :n_exp].set(router_w[i])
        wr_hi = wr.astype(BF16)
        wr_lo = (wr - wr_hi.astype(F32)).astype(BF16)
        br = jnp.full((1, LANES), -1e30, F32).at[0, :n_exp].set(router_b[i])
        if i % 2 == 0:
            w_in = gdn_w_in[j]
            w_main = w_in[:, :qkv_w + qk_w].astype(BF16)
            n_ab = w_in.shape[1] - (qkv_w + qk_w)
            w_ab = jnp.zeros((d, LANES), F32).at[:, :n_ab].set(w_in[:, qkv_w + qk_w:]).astype(BF16)
            ab_c = jnp.zeros((8, LANES), F32)
            ab_c = ab_c.at[0, :2 * GDN_HEADS].set(-jnp.exp(gdn_a_log[j].reshape(-1)))
            ab_c = ab_c.at[1, :2 * GDN_HEADS].set(gdn_dt_bias[j].reshape(-1))
            qkv, z, gb = pl.pallas_call(
                functools.partial(_gdn_in_kernel, qkv_w=qkv_w),
                grid=(b, n_bt),
                in_specs=[tok_spec(d), mod_spec, full((1, d)), full((d, qkv_w + qk_w)), full((d, LANES)),
                          full((8, LANES))],
                out_specs=[tok_spec(qkv_w), tok_spec(qk_w), tok_spec(LANES)],
                out_shape=[tok_shape(qkv_w, BF16), tok_shape(qk_w, BF16), tok_shape(LANES, F32)],
                compiler_params=_cparams(("parallel", "parallel")),
                name="gdn_in",
            )(xs, mods_l, g1, w_main, w_ab, ab_c)

            qkv_c = pl.pallas_call(
                functools.partial(_gdn_conv_kernel, n_ctx=n_ctx, n_lat=t, qk_blocks=2 * GDN_HEADS),
                grid=(b, qkv_w // LANES),
                in_specs=[pl.BlockSpec((None, s, LANES), lambda bi, ci: (bi, 0, ci)),
                          pl.BlockSpec((GDN_CONV, LANES), lambda bi, ci: (0, ci))],
                out_specs=pl.BlockSpec((None, s, LANES), lambda bi, ci: (bi, 0, ci)),
                out_shape=tok_shape(qkv_w, BF16),
                scratch_shapes=[pltpu.VMEM((s + 24, LANES), F32)],
                compiler_params=_cparams(("parallel", "parallel")),
                name="gdn_conv",
            )(qkv, gdn_conv_w[j])

            gbt = jnp.swapaxes(gb[:, :, :4 * GDN_HEADS], 1, 2)

            def bwd_tile(ti):
                return jnp.where(ti < n_ct, n_ct - 1 - ti, n_bt - 1 - (ti - n_ct))

            hw = SCAN_HEADS * GDN_DK
            n_hb = GDN_HEADS // SCAN_HEADS

            def hd_spec(off, tmap):
                return pl.BlockSpec((None, TOK_TILE, hw), lambda bi, hi, ti: (bi, tmap(ti), off + hi))

            fwd_tile = lambda ti: ti
            scan_in = []
            for tmap in (fwd_tile, bwd_tile):
                scan_in += [hd_spec(0, tmap), hd_spec(n_hb, tmap), hd_spec(2 * n_hb, tmap),
                            pl.BlockSpec((None, TOK_TILE, LANES), lambda bi, hi, ti, tmap=tmap: (bi, tmap(ti), 0)),
                            pl.BlockSpec((None, 4 * GDN_HEADS, TOK_TILE),
                                         lambda bi, hi, ti, tmap=tmap: (bi, 0, tmap(ti)))]
            o_f, o_b = pl.pallas_call(
                functools.partial(_gdn_scan_kernel, scale=GDN_DK ** -0.5),
                grid=(b, n_hb, n_bt),
                in_specs=scan_in,
                out_specs=[hd_spec(0, fwd_tile), hd_spec(0, bwd_tile)],
                out_shape=[tok_shape(qk_w, BF16), tok_shape(qk_w, BF16)],
                scratch_shapes=[pltpu.VMEM((SCAN_HEADS, GDN_DK, GDN_DK), F32),
                                pltpu.VMEM((SCAN_HEADS, GDN_DK, GDN_DK), F32)],
                compiler_params=_cparams(("parallel", "parallel", "arbitrary")),
                name="gdn_scan",
            )(qkv_c, qkv_c, qkv_c, gb, gbt, qkv_c, qkv_c, qkv_c, gb, gbt)

            xs, h2, route, counts = pl.pallas_call(
                _gdn_out_kernel,
                grid=(b, n_t),
                in_specs=[tok_spec_o(qk_w), tok_spec_o(qk_w), tok_spec_o(qk_w), tok_spec_o(d), mod_spec_o,
                          full((1, LANES)), full((qk_w, d)), full((1, d)), full((d, LANES)), full((d, LANES)),
                          full((1, LANES))],
                out_specs=route_out_specs,
                out_shape=route_out_shapes,
                input_output_aliases={3: 0},
                compiler_params=_cparams(("parallel", "parallel")),
                name="gdn_out",
            )(o_f, o_b, z, xs, mods_l, gdn_norm_g[j].reshape(1, LANES), gdn_w_out[j].astype(BF16), g2,
              wr_hi, wr_lo, br)
        else:
            width = gmlp_w_out.shape[1]
            xs, h2, route, counts = pl.pallas_call(
                _gmlp_kernel,
                grid=(b, n_t),
                in_specs=[tok_spec_o(d), mod_spec_o, full((1, d)), full((d, 2 * width)), full((1, 2 * width)),
                          full((1, width)), full((1, width)), full((GMLP_GROUPS, GMLP_CHUNK, GMLP_CHUNK)),
                          full((GMLP_CHUNK, GMLP_GROUPS)), full((width, d)), full((1, d)), full((d, LANES)),
                          full((d, LANES)), full((1, LANES))],
                out_specs=route_out_specs,
                out_shape=route_out_shapes,
                input_output_aliases={0: 0},
                compiler_params=_cparams(("parallel", "parallel")),
                name="gmlp",
            )(xs, mods_l, g1, gmlp_w_in[j].astype(BF16), gmlp_b_in[j].reshape(1, -1),
              gmlp_ln_g[j].reshape(1, -1), gmlp_ln_b[j].reshape(1, -1), gmlp_w_s[j].astype(BF16),
              gmlp_b_s[j].T, gmlp_w_out[j].astype(BF16), g2, wr_hi, wr_lo, br)

        xs = _moe(xs, h2, route, counts, mods_l, mod_map, i, moe_w_gu, moe_b_gu, moe_w_dn, moe_b_dn, n_bt, t0)

    return pl.pallas_call(
        _final_kernel,
        grid=(b, t // TOK_TILE),
        in_specs=[pl.BlockSpec((None, TOK_TILE, d), lambda bi, ti: (bi, n_ct + ti, 0)), full((1, d))],
        out_specs=pl.BlockSpec((None, TOK_TILE, d), lambda bi, ti: (bi, ti, 0)),
        out_shape=jax.ShapeDtypeStruct((b, t, d), x.dtype),
        compiler_params=_cparams(("parallel", "parallel")),
        name="final_norm",
    )(xs, final_g.reshape(1, d))
```

```python
import functools

import jax
import jax.numpy as jnp
from jax import lax
from jax.experimental import pallas as pl
from jax.experimental.pallas import tpu as pltpu

F32 = jnp.float32
BF16 = jnp.bfloat16

NORM_EPS = 1e-6
N_ADA = 6
GDN_HEADS = 8
GDN_DK = 128
GDN_CONV = 5
GDN_CHUNK = 64
GMLP_CHUNK = 128
GMLP_GROUPS = 8
TOP_K = 4
SWIGLU_ALPHA = 1.702
SWIGLU_LIMIT = 7.0

LANES = 128
SUBLANES = 8
TOK_TILE = 256
MOE_TILE = 512
LONG_RUN = 64
INV_BLOCK = 16
SCAN_HEADS = 8
VMEM_LIMIT = 56 * 1024 * 1024


def _cparams(sem):
    return pltpu.CompilerParams(dimension_semantics=sem, vmem_limit_bytes=VMEM_LIMIT)


def _dot(a, b):
    return jnp.dot(a.astype(BF16), b.astype(BF16), preferred_element_type=F32)


def _dot_nt(a, b):
    return lax.dot_general(a.astype(BF16), b.astype(BF16), (((1,), (1,)), ((), ())),
                           preferred_element_type=F32)


def _dot_tn(a, b):
    return lax.dot_general(a.astype(BF16), b.astype(BF16), (((0,), (0,)), ((), ())),
                           preferred_element_type=F32)


def _split(a):
    hi = a.astype(BF16)
    lo = (a - hi.astype(F32)).astype(BF16)
    return hi, lo


def _dot3(a, b):
    ah, al = _split(a)
    bh, bl = _split(b)
    d = lambda x, y: jnp.dot(x, y, preferred_element_type=F32)
    return d(ah, bh) + (d(ah, bl) + d(al, bh))


def _rms_scale(xf):
    return xf * lax.rsqrt(jnp.mean(xf * xf, axis=-1, keepdims=True) + NORM_EPS)


def _silu(x):
    return x * jax.nn.sigmoid(x)


def _mod_part(m, k, d):
    return m[:, k * d:(k + 1) * d]


def _ada_kernel(cc_ref, w_ref, b_ref, o_ref):
    s = _silu(cc_ref[...])
    o_ref[...] = _dot3(s, w_ref[...]) + b_ref[...]


def _ada_mods(cc, ada_w, ada_b):
    depth, d, n = ada_w.shape
    r = cc.shape[0]
    tn = n // 4
    return pl.pallas_call(
        _ada_kernel,
        grid=(depth, n // tn),
        in_specs=[pl.BlockSpec((r, d), lambda l, j: (0, 0)),
                  pl.BlockSpec((None, d, tn), lambda l, j: (l, 0, j)),
                  pl.BlockSpec((None, 1, tn), lambda l, j: (l, 0, j))],
        out_specs=pl.BlockSpec((None, r, tn), lambda l, j: (l, 0, j)),
        out_shape=jax.ShapeDtypeStruct((depth, r, n), F32),
        compiler_params=_cparams(("parallel", "parallel")),
        name="ada_mods",
    )(cc, ada_w, ada_b.reshape(depth, 1, n))


def _route_epilogue(x_new, m, g2, wr_hi, wr_lo, br, tri, h2_ref, route_ref, cnt_ref):
    d = x_new.shape[-1]
    tm = x_new.shape[0]
    h2 = (_rms_scale(x_new) * g2) * (1.0 + _mod_part(m, 4, d)) + _mod_part(m, 3, d)
    h2_ref[...] = h2.astype(h2_ref.dtype)
    hh, hl = _split(h2)
    dd = lambda a, b: jnp.dot(a, b, preferred_element_type=F32)
    logits = dd(hh, wr_hi) + (dd(hh, wr_lo) + dd(hl, wr_hi)) + br
    lane = lax.broadcasted_iota(jnp.int32, (tm, LANES), 1).astype(F32)
    vals, idxs = [], []
    l = logits
    for _ in range(TOP_K):
        mx = jnp.max(l, axis=1, keepdims=True)
        ix = jnp.min(jnp.where(l == mx, lane, float(LANES)), axis=1, keepdims=True)
        vals.append(mx)
        idxs.append(ix)
        l = jnp.where(lane == ix, -jnp.inf, l)
    es = [jnp.exp(v - vals[0]) for v in vals]
    den = es[0] + es[1] + es[2] + es[3]
    hots = [(lane == ix).astype(F32) for ix in idxs]
    oh4 = hots[0] + hots[1] + hots[2] + hots[3]
    prefix = jnp.dot(tri, oh4.astype(BF16), preferred_element_type=F32)
    route = jnp.zeros((tm, LANES), F32)
    for k in range(TOP_K):
        rank = jnp.sum(hots[k] * prefix, axis=1, keepdims=True)
        route = jnp.where(lane == float(k), idxs[k], route)
        route = jnp.where(lane == float(TOP_K + k), es[k] / den, route)
        route = jnp.where(lane == float(2 * TOP_K + k), rank, route)
    route_ref[...] = route
    cnt_ref[...] = jnp.sum(oh4, axis=0, keepdims=True)


def _strict_lower(n):
    i = lax.broadcasted_iota(jnp.int32, (n, n), 0)
    j = lax.broadcasted_iota(jnp.int32, (n, n), 1)
    return (j < i).astype(BF16)


def _gdn_in_kernel(x_ref, m_ref, g1_ref, w_ref, wab_ref, ab_ref, qkv_ref, z_ref, gb_ref, *, qkv_w):
    d = x_ref.shape[-1]
    m = m_ref[...]
    h = (_rms_scale(x_ref[...]) * g1_ref[...]) * (1.0 + _mod_part(m, 1, d)) + _mod_part(m, 0, d)
    hb = h.astype(BF16)
    p = jnp.dot(hb, w_ref[...], preferred_element_type=F32)
    qkv_ref[...] = p[:, :qkv_w].astype(BF16)
    z_ref[...] = p[:, qkv_w:].astype(BF16)
    ab = jnp.dot(hb, wab_ref[...], preferred_element_type=F32)
    neg_a = ab_ref[0:1, :]
    dt_b = ab_ref[1:2, :]
    xs = ab + dt_b
    softplus = jnp.maximum(xs, 0.0) + jnp.log(1.0 + jnp.exp(-jnp.abs(xs)))
    lane = lax.broadcasted_iota(jnp.int32, ab.shape, 1)
    gb_ref[...] = jnp.where(lane < 2 * GDN_HEADS, neg_a * softplus, jax.nn.sigmoid(ab))


def _gdn_conv_kernel(x_ref, w_ref, o_ref, pad_ref, *, n_ctx, n_lat, qk_blocks):
    tile = TOK_TILE
    zeros8 = jnp.zeros((8, LANES), F32)
    pad_ref[0:8, :] = zeros8
    pad_ref[8 + n_ctx:16 + n_ctx, :] = zeros8
    pad_ref[16 + n_ctx + n_lat:24 + n_ctx + n_lat, :] = zeros8
    for r in range(0, n_ctx + n_lat, tile):
        base = 8 + r if r < n_ctx else 16 + r
        pad_ref[base:base + tile, :] = x_ref[r:r + tile, :].astype(F32)
    w = w_ref[...]
    is_qk = pl.program_id(1) < qk_blocks
    half = GDN_CONV // 2
    for r in range(0, n_ctx + n_lat, tile):
        base = 8 + r if r < n_ctx else 16 + r
        acc = jnp.zeros((tile, LANES), F32)
        for k in range(GDN_CONV):
            s = base + k - half
            acc = acc + pad_ref[s:s + tile, :] * w[k:k + 1, :]
        y = _silu(acc)
        yn = y * lax.rsqrt(jnp.sum(y * y, axis=-1, keepdims=True) + NORM_EPS)
        o_ref[r:r + tile, :] = jnp.where(is_qk, yn, y).astype(BF16)


def _each(fn, *seqs):
    return [fn(*args) for args in zip(*seqs)]


def _unit_tri_inverse(lmats, bi, bj):
    n = lmats[0].shape[0]
    eye = (lax.broadcasted_iota(jnp.int32, (n, n), 0) == lax.broadcasted_iota(jnp.int32, (n, n), 1)).astype(F32)
    ld = _each(lambda l: jnp.where(bi == bj, l, 0.0), lmats)
    x = _each(lambda l: eye - l, ld)
    p = _each(_dot, ld, ld)
    width = 2
    while True:
        x = _each(lambda xi, pi: xi + _dot(xi, pi), x, p)
        width *= 2
        if width >= INV_BLOCK:
            break
        p = _each(_dot, p, p)
    span = 2
    while INV_BLOCK * span <= GDN_CHUNK:
        half = span // 2
        off = ((bi // span) == (bj // span)) & ((bi // half) != (bj // half))
        xc = _each(lambda xi, l: _dot(xi, jnp.where(off, l, 0.0)), x, lmats)
        x = _each(lambda xi, xci: xi - _dot(xci, xi), x, xc)
        span *= 2
    return x


def _gdn_scan_kernel(qf_ref, kf_ref, vf_ref, gbf_ref, gtf_ref, qb_ref, kb_ref, vb_ref, gbb_ref, gtb_ref,
                     of_ref, ob_ref, sf_ref, sb_ref, *, scale):
    h0 = pl.program_id(1) * (qf_ref.shape[1] // GDN_DK)

    @pl.when(pl.program_id(2) == 0)
    def _():
        sf_ref[...] = jnp.zeros_like(sf_ref)
        sb_ref[...] = jnp.zeros_like(sb_ref)

    tm = qf_ref.shape[0]
    dk = GDN_DK
    hb = qf_ref.shape[1] // dk
    cs = GDN_CHUNK
    n_chunks = tm // cs
    lane = lax.broadcasted_iota(jnp.int32, (tm, LANES), 1)
    ii = lax.broadcasted_iota(jnp.int32, (tm, tm), 0)
    jj = lax.broadcasted_iota(jnp.int32, (tm, tm), 1)
    same = (ii // cs) == (jj // cs)
    bi, bj = ii // INV_BLOCK, jj // INV_BLOCK

    def col(gb, idx):
        return jnp.sum(jnp.where(lane == idx, gb, 0.0), axis=1, keepdims=True)

    probs = [(hh, d) for hh in range(hb) for d in range(2)]
    tri_incl = [same & (ii >= jj), same & (ii <= jj)]
    tri_strict = [same & (ii > jj), same & (ii < jj)]
    incl = [tri_incl[d] for _, d in probs]
    incl_t = [tri_incl[1 - d] for _, d in probs]
    strict = [tri_strict[d] for _, d in probs]
    gbs = [gbf_ref[...], gbb_ref[...]]
    gts = [gtf_ref, gtb_ref]
    qkv_refs = [(qf_ref, kf_ref, vf_ref), (qb_ref, kb_ref, vb_ref)]
    o_refs = [of_ref, ob_ref]
    s_refs = [sf_ref, sb_ref]
    g_col = [col(gbs[d], d * GDN_HEADS + h0 + hh) for hh, d in probs]
    beta = [col(gbs[d], (2 + d) * GDN_HEADS + h0 + hh) for hh, d in probs]
    g_row = [gts[d][pl.ds(d * GDN_HEADS + h0 + hh, 1), :] for hh, d in probs]
    q, k, v = [[qkv_refs[d][j][:, hh * dk:(hh + 1) * dk].astype(F32) for hh, d in probs] for j in range(3)]

    gc_col = _each(lambda m, g: jnp.sum(jnp.where(m, g, 0.0), axis=1, keepdims=True), incl, g_row)
    gc_row = _each(lambda m, g: jnp.sum(jnp.where(m, g, 0.0), axis=0, keepdims=True), incl_t, g_col)
    g_tot = _each(lambda g: jnp.sum(jnp.where(same, g, 0.0), axis=1, keepdims=True), g_row)
    decay = _each(lambda m, c, r: jnp.exp(jnp.where(m, c - r, -jnp.inf)), incl, gc_col, gc_row)
    qkk = _each(lambda qi, ki: _dot_nt(jnp.concatenate([qi, ki], axis=0), ki), q, k)
    lmat = _each(lambda m, b, x, dc: jnp.where(m, b * x[tm:] * dc, 0.0), strict, beta, qkk, decay)
    t_inv = _unit_tri_inverse(lmat, bi, bj)
    eg = _each(jnp.exp, gc_col)
    wu = _each(lambda t, ki, vi, b, e: _dot(t, jnp.concatenate([ki * (b * e), vi * b], axis=1)),
               t_inv, k, v, beta, eg)
    awu = _each(lambda x, dc, w: _dot(x[:tm] * (scale * dc), w), qkk, decay, wu)
    q_eff = _each(lambda qi, e, a: qi * (scale * e) - a[:, :dk], q, eg, awu)
    k_tail = _each(lambda ki, t, c: ki * jnp.exp(t - c), k, g_tot, gc_col)
    s = [s_refs[d][hh] for hh, d in probs]
    for step in range(n_chunks):
        for p, (hh, d) in enumerate(probs):
            c = step if d == 0 else n_chunks - 1 - step
            sl = slice(c * cs, (c + 1) * cs)
            kwu = _dot_tn(k_tail[p][sl], wu[p][sl])
            o_refs[d][sl, hh * dk:(hh + 1) * dk] = (
                _dot(q_eff[p][sl], s[p]) + awu[p][sl, dk:]).astype(o_refs[d].dtype)
            s[p] = s[p] * jnp.exp(g_tot[p][c * cs:c * cs + 1, :]) + (kwu[:, dk:] - _dot(kwu[:, :dk], s[p]))
    for p, (hh, d) in enumerate(probs):
        s_refs[d][hh] = s[p]


def _gdn_out_kernel(of_ref, ob_ref, z_ref, x_ref, m_ref, ng_ref, w_ref, g2_ref, wrh_ref, wrl_ref, br_ref,
                    xo_ref, h2_ref, route_ref, cnt_ref):
    d = x_ref.shape[-1]
    m = m_ref[...]
    o = of_ref[...].astype(F32) + ob_ref[...].astype(F32)
    z = z_ref[...].astype(F32)
    ng = ng_ref[...]
    parts = []
    for hd in range(o.shape[1] // LANES):
        sl = slice(hd * LANES, (hd + 1) * LANES)
        parts.append(_rms_scale(o[:, sl]) * ng * _silu(z[:, sl]))
    y = jnp.dot(jnp.concatenate(parts, axis=1).astype(BF16), w_ref[...], preferred_element_type=F32)
    x_new = x_ref[...] + _mod_part(m, 2, d) * y
    xo_ref[...] = x_new
    _route_epilogue(x_new, m, g2_ref[...], wrh_ref[...], wrl_ref[...], br_ref[...],
                    _strict_lower(x_new.shape[0]), h2_ref, route_ref, cnt_ref)


def _gmlp_kernel(x_ref, m_ref, g1_ref, win_ref, bin_ref, lng_ref, lnb_ref, ws_ref, bs_ref, wout_ref,
                 g2_ref, wrh_ref, wrl_ref, br_ref, xo_ref, h2_ref, route_ref, cnt_ref):
    d = x_ref.shape[-1]
    tm = x_ref.shape[0]
    m = m_ref[...]
    x = x_ref[...]
    width = wout_ref.shape[0]
    gw = width // GMLP_GROUPS
    gelu = lambda a: 0.5 * a * (1.0 + lax.erf(a * (2.0 ** -0.5)))
    xc = [x[c * GMLP_CHUNK:(c + 1) * GMLP_CHUNK] for c in range(tm // GMLP_CHUNK)]
    hb = _each(lambda a: ((_rms_scale(a) * g1_ref[...]) * (1.0 + _mod_part(m, 1, d))
                          + _mod_part(m, 0, d)).astype(BF16), xc)
    v = _each(lambda a: gelu(jnp.dot(a, win_ref[:, width:], preferred_element_type=F32) + bin_ref[:, width:]), hb)
    vc = _each(lambda a: a - jnp.mean(a, axis=-1, keepdims=True), v)
    vb = _each(lambda a: ((a * lax.rsqrt(jnp.mean(a * a, axis=-1, keepdims=True) + NORM_EPS)) * lng_ref[...]
                          + lnb_ref[...]).astype(BF16), vc)
    s = _each(lambda a: jnp.concatenate(
        [jnp.dot(ws_ref[g], a[:, g * gw:(g + 1) * gw], preferred_element_type=F32) + bs_ref[:, g:g + 1]
         for g in range(GMLP_GROUPS)], axis=1), vb)
    u = _each(lambda a: gelu(jnp.dot(a, win_ref[:, :width], preferred_element_type=F32) + bin_ref[:, :width]), hb)
    y = _each(lambda a, b: jnp.dot((a * b).astype(BF16), wout_ref[...], preferred_element_type=F32), u, s)
    x_new = x + _mod_part(m, 2, d) * jnp.concatenate(y, axis=0)
    xo_ref[...] = x_new
    _route_epilogue(x_new, m, g2_ref[...], wrh_ref[...], wrl_ref[...], br_ref[...],
                    _strict_lower(tm), h2_ref, route_ref, cnt_ref)


def _local_positions(route, loffv):
    tm = route.shape[0]
    lane = lax.broadcasted_iota(jnp.int32, (tm, LANES), 1).astype(F32)
    out = []
    for k in range(TOP_K):
        first = jnp.sum(jnp.where(lane == route[:, k:k + 1], loffv, 0.0), axis=1, keepdims=True)
        out.append(first + route[:, 2 * TOP_K + k:2 * TOP_K + k + 1])
    return out


def _rows(first, count):
    return pl.ds(pl.multiple_of(first * SUBLANES, SUBLANES), count * SUBLANES)


def _lane_block(first, count, sb):
    return pl.ds(first * SUBLANES + sb, count, stride=SUBLANES)


def _run_pieces(cnt, fn):
    def pieces(sizes):
        for size in sizes:
            def piece(size=size):
                fn(cnt & ~(2 * size - 1), size)
            pl.when((cnt & size) != 0)(piece)

    sizes = [TOK_TILE >> sh for sh in range(TOK_TILE.bit_length())]
    pl.when(cnt >= LONG_RUN)(lambda: pieces([sz for sz in sizes if sz >= LONG_RUN]))
    pieces([sz for sz in sizes if sz < LONG_RUN])


def _dispatch_kernel(cnt_s, loff_s, base_s, pstart_s, plen_s, h_ref, route_ref, loffv_ref, o_ref, buf_ref, sem):
    i = pl.program_id(0)
    n_exp = plen_s.shape[0]
    tm = h_ref.shape[0]
    n_pairs = tm * TOP_K
    slot = i % 2
    buf = buf_ref.at[slot]
    lpos = _local_positions(route_ref[...], loffv_ref[...])
    ii = lax.broadcasted_iota(jnp.int32, (tm, tm), 0)
    jj = lax.broadcasted_iota(jnp.int32, (tm, tm), 1)
    lpos_rows = [jnp.sum(jnp.where(ii == jj, lp, 0.0), axis=0, keepdims=True) for lp in lpos]
    hb = h_ref[...]
    for blk in range(TOP_K):
        pair = (ii + blk * tm).astype(F32)
        perm = jnp.zeros((tm, tm), F32)
        for k in range(TOP_K):
            perm = perm + (pair == lpos_rows[k]).astype(F32)
        hp = jnp.dot(perm.astype(BF16), hb, preferred_element_type=F32)
        for sb in range(hp.shape[1] // LANES):
            buf[_lane_block(blk * tm, tm, sb), :] = hp[:, sb * LANES:(sb + 1) * LANES]

    def start_run(e, carry):
        lo, bs = loff_s[i * n_exp + e], base_s[i * n_exp + e]
        _run_pieces(cnt_s[i * n_exp + e], lambda off, size: pltpu.make_async_copy(
            buf.at[_rows(lo + off, size)], o_ref.at[_rows(bs + off, size)], sem.at[slot]).start())
        return carry

    lax.fori_loop(0, n_exp, start_run, 0)

    def wait_tile(sl):
        pltpu.make_async_copy(buf_ref.at[sl], o_ref.at[_rows(0, n_pairs)], sem.at[sl]).wait()

    pl.when(i > 0)(lambda: wait_tile(1 - slot))

    @pl.when(i == pl.num_programs(0) - 1)
    def _():
        wait_tile(slot)
        buf[0:MOE_TILE * SUBLANES, :] = jnp.zeros((MOE_TILE * SUBLANES, LANES), F32)

        def pad_copy(e, off, size):
            return pltpu.make_async_copy(buf.at[_rows(0, size)], o_ref.at[_rows(pstart_s[e] + off, size)],
                                         sem.at[slot])

        def fill(e, carry):
            _run_pieces(plen_s[e], lambda off, size: pad_copy(e, off, size).start())
            _run_pieces(plen_s[e], lambda off, size: pad_copy(e, off, size).wait())
            return carry

        lax.fori_loop(0, n_exp, fill, 0)


def _expert_kernel(te_ref, nt_ref, x_ref, wgu_ref, bgu_ref, wdn_ref, bdn_ref, o_ref, xs_ref, wgu_bf, wdn_bf):
    j = pl.program_id(0)
    n_rows = xs_ref.shape[0]

    @pl.when((j == 0) | (te_ref[j] != te_ref[jnp.maximum(j - 1, 0)]))
    def _():
        wgu_bf[...] = wgu_ref[...].astype(BF16)
        wdn_bf[...] = wdn_ref[...].astype(BF16)

    @pl.when(j < nt_ref[0])
    def _():
        de = wdn_ref.shape[0]
        for sb in range(SUBLANES):
            xs_ref[:, sb * LANES:(sb + 1) * LANES] = x_ref[_lane_block(0, n_rows, sb), :].astype(BF16)
        gu = jnp.dot(xs_ref[...], wgu_bf[...], preferred_element_type=F32) + bgu_ref[...]
        gate = jnp.minimum(gu[:, :de], SWIGLU_LIMIT)
        up = jnp.clip(gu[:, de:], -SWIGLU_LIMIT, SWIGLU_LIMIT)
        glu = gate * jax.nn.sigmoid(gate * SWIGLU_ALPHA)
        act = ((up + 1.0) * glu).astype(BF16)
        y = jnp.dot(act, wdn_bf[...], preferred_element_type=F32) + bdn_ref[...]
        for sb in range(SUBLANES):
            o_ref[_lane_block(0, n_rows, sb), :] = y[:, sb * LANES:(sb + 1) * LANES]

    @pl.when(pl.program_id(0) >= nt_ref[0])
    def _():
        o_ref[...] = jnp.zeros_like(o_ref)


def _combine_kernel(cnt_s, loff_s, base_s, x_ref, route_ref, loffv_ref, m_ref, y_ref, *rest, final_norm):
    if final_norm:
        fg_ref, o_ref, buf_ref, sem = rest
    else:
        o_ref, buf_ref, sem = rest
    i = pl.program_id(0)
    n_exp = cnt_s.shape[0] // pl.num_programs(0)
    tm, d = x_ref.shape
    n_pairs = tm * TOP_K
    slot = i % 2

    def fetch(tile, sl):
        def start_run(e, carry):
            lo, bs = loff_s[tile * n_exp + e], base_s[tile * n_exp + e]
            _run_pieces(cnt_s[tile * n_exp + e], lambda off, size: pltpu.make_async_copy(
                y_ref.at[_rows(bs + off, size)], buf_ref.at[sl].at[_rows(lo + off, size)], sem.at[sl]).start())
            return carry

        lax.fori_loop(0, n_exp, start_run, 0)

    pl.when(i == 0)(lambda: fetch(i, slot))
    pl.when(i + 1 < pl.num_programs(0))(lambda: fetch(i + 1, 1 - slot))
    buf = buf_ref.at[slot]
    route = route_ref[...]
    lpos = _local_positions(route, loffv_ref[...])
    jj = lax.broadcasted_iota(jnp.int32, (tm, tm), 1)
    combs = []
    for blk in range(TOP_K):
        pair = (jj + blk * tm).astype(F32)
        comb = jnp.zeros((tm, tm), F32)
        for k in range(TOP_K):
            comb = comb + jnp.where(pair == lpos[k], route[:, TOP_K + k:TOP_K + k + 1], 0.0)
        combs.append(comb.astype(BF16))
    pltpu.make_async_copy(y_ref.at[_rows(0, n_pairs)], buf, sem.at[slot]).wait()
    acc = jnp.zeros((tm, d), F32)
    for blk in range(TOP_K):
        y = jnp.concatenate([buf[_lane_block(blk * tm, tm, sb), :].astype(BF16) for sb in range(SUBLANES)],
                            axis=1)
        acc = acc + jnp.dot(combs[blk], y, preferred_element_type=F32)
    x_new = x_ref[...] + _mod_part(m_ref[...], 5, d) * acc
    o_ref[...] = _rms_scale(x_new) * fg_ref[...] if final_norm else x_new


def _final_kernel(x_ref, g_ref, o_ref):
    o_ref[...] = _rms_scale(x_ref[...]) * g_ref[...]


def _moe(xs, h2, route, counts, mods_l, mod_map, layer, w_gu, b_gu, w_dn, b_dn, n_bt, t0, final_g=None):
    b, s, d = xs.shape
    n = b * s
    n_exp = w_gu.shape[1]
    n_t = n_bt - t0
    nt_tok = b * n_t
    assert (nt_tok * TOK_TILE * TOP_K) % MOE_TILE == 0
    max_tiles = (nt_tok * TOK_TILE * TOP_K) // MOE_TILE + n_exp
    n_pairs = TOK_TILE * TOP_K
    tok_tile = lambda i: (i // n_t) * n_bt + t0 + i % n_t
    cnt = counts.reshape(nt_tok, LANES)[:, :n_exp].astype(jnp.int32)
    totals = jnp.sum(cnt, axis=0)
    tiles_e = (totals + MOE_TILE - 1) // MOE_TILE
    tile_end = jnp.cumsum(tiles_e)
    starts = (tile_end - tiles_e) * MOE_TILE
    base = starts[None, :] + jnp.cumsum(cnt, axis=0) - cnt
    loff = jnp.cumsum(cnt, axis=1) - cnt
    loffv = jnp.zeros((nt_tok, 1, LANES), F32).at[:, 0, :n_exp].set(loff.astype(F32))
    tile_ids = jnp.arange(max_tiles, dtype=jnp.int32)
    tile_expert = jnp.minimum(jnp.sum((tile_end[None, :] <= tile_ids[:, None]).astype(jnp.int32), axis=1),
                              n_exp - 1)
    n_tiles = tile_end[-1:].astype(jnp.int32)
    plan = (cnt.reshape(-1), loff.reshape(-1), base.reshape(-1))
    sorted_shape = jax.ShapeDtypeStruct((max_tiles * MOE_TILE * SUBLANES, LANES), F32)

    hs = pl.pallas_call(
        _dispatch_kernel,
        grid_spec=pltpu.PrefetchScalarGridSpec(
            num_scalar_prefetch=5,
            grid=(nt_tok,),
            in_specs=[pl.BlockSpec((TOK_TILE, d), lambda i, *_: (tok_tile(i), 0)),
                      pl.BlockSpec((TOK_TILE, LANES), lambda i, *_: (tok_tile(i), 0)),
                      pl.BlockSpec((None, 1, LANES), lambda i, *_: (i, 0, 0))],
            out_specs=pl.BlockSpec(memory_space=pl.ANY),
            scratch_shapes=[pltpu.VMEM((2, n_pairs * SUBLANES, LANES), F32), pltpu.SemaphoreType.DMA((2,))]),
        out_shape=sorted_shape,
        compiler_params=_cparams(("arbitrary",)),
        name="moe_dispatch",
    )(*plan, starts + totals, tiles_e * MOE_TILE - totals, h2.reshape(n, d), route.reshape(n, LANES), loffv)

    de = w_dn.shape[2]
    ys = pl.pallas_call(
        _expert_kernel,
        grid_spec=pltpu.PrefetchScalarGridSpec(
            num_scalar_prefetch=2,
            grid=(max_tiles,),
            in_specs=[pl.BlockSpec((MOE_TILE * SUBLANES, LANES), lambda j, te, nt: (jnp.where(j < nt[0], j, 0), 0)),
                      pl.BlockSpec((None, None, d, 2 * de), lambda j, te, nt: (layer, te[j], 0, 0)),
                      pl.BlockSpec((None, None, 1, 2 * de), lambda j, te, nt: (layer, te[j], 0, 0)),
                      pl.BlockSpec((None, None, de, d), lambda j, te, nt: (layer, te[j], 0, 0)),
                      pl.BlockSpec((None, None, 1, d), lambda j, te, nt: (layer, te[j], 0, 0))],
            out_specs=pl.BlockSpec((MOE_TILE * SUBLANES, LANES), lambda j, te, nt: (j, 0)),
            scratch_shapes=[pltpu.VMEM((MOE_TILE, d), BF16), pltpu.VMEM((d, 2 * de), BF16),
                            pltpu.VMEM((de, d), BF16)]),
        out_shape=sorted_shape,
        compiler_params=_cparams(("arbitrary",)),
        name="moe_experts",
    )(tile_expert, n_tiles, hs, w_gu, b_gu.reshape(-1, n_exp, 1, 2 * de), w_dn, b_dn.reshape(-1, n_exp, 1, d))

    final_norm = final_g is not None
    out = pl.pallas_call(
        functools.partial(_combine_kernel, final_norm=final_norm),
        grid_spec=pltpu.PrefetchScalarGridSpec(
            num_scalar_prefetch=3,
            grid=(nt_tok,),
            in_specs=[pl.BlockSpec((TOK_TILE, d), lambda i, *_: (tok_tile(i), 0)),
                      pl.BlockSpec((TOK_TILE, LANES), lambda i, *_: (tok_tile(i), 0)),
                      pl.BlockSpec((None, 1, LANES), lambda i, *_: (i, 0, 0)),
                      pl.BlockSpec((None, 1, N_ADA * d), lambda i, *_: (mod_map(i // n_t, t0 + i % n_t), 0, 0)),
                      pl.BlockSpec(memory_space=pl.ANY)]
            + ([pl.BlockSpec((1, d), lambda i, *_: (0, 0))] if final_norm else []),
            out_specs=pl.BlockSpec((TOK_TILE, d), lambda i, *_: ((i if final_norm else tok_tile(i)), 0)),
            scratch_shapes=[pltpu.VMEM((2, n_pairs * SUBLANES, LANES), F32), pltpu.SemaphoreType.DMA((2,))]),
        out_shape=jax.ShapeDtypeStruct((nt_tok * TOK_TILE if final_norm else n, d), F32),
        input_output_aliases={} if final_norm else {3: 0},
        compiler_params=_cparams(("arbitrary",)),
        name="moe_combine",
    )(*plan, xs.reshape(n, d), route.reshape(n, LANES), loffv, mods_l, ys,
      *([final_g.reshape(1, d)] if final_norm else []))
    return out.reshape(b, -1, d)


def kernel(x, c, ctx, c_ctx, ada_w, ada_b, norm1_g, norm2_g, gdn_w_in, gdn_conv_w, gdn_a_log, gdn_dt_bias, gdn_norm_g, gdn_w_out, gmlp_w_in, gmlp_b_in, gmlp_ln_g, gmlp_ln_b, gmlp_w_s, gmlp_b_s, gmlp_w_out, router_w, router_b, moe_w_gu, moe_b_gu, moe_w_dn, moe_b_dn, final_g):
    b, t, d = x.shape
    n_ctx = ctx.shape[1]
    s = n_ctx + t
    depth = ada_w.shape[0]
    n_exp = router_w.shape[-1]
    assert t % TOK_TILE == 0 and n_ctx % TOK_TILE == 0 and d % LANES == 0 and n_exp <= LANES
    assert (b * s * TOP_K) % MOE_TILE == 0 and d == SUBLANES * LANES
    n_bt = s // TOK_TILE
    n_ct = n_ctx // TOK_TILE
    qk_w = GDN_HEADS * GDN_DK
    qkv_w = 3 * qk_w

    n_rows = -(-(b + 1) // 8) * 8
    cc = jnp.zeros((n_rows, d), F32).at[:b].set(c).at[b].set(c_ctx)
    mods = _ada_mods(cc, ada_w, ada_b).reshape(depth, n_rows, 1, N_ADA * d)

    def mod_map(bi, ti):
        return jnp.where(ti < n_ct, b, bi)

    mod_spec_from = lambda t0: pl.BlockSpec((None, 1, N_ADA * d), lambda bi, ti: (mod_map(bi, ti + t0), 0, 0))
    tok_spec_from = lambda t0: (lambda w: pl.BlockSpec((None, TOK_TILE, w), lambda bi, ti: (bi, ti + t0, 0)))
    mod_spec, tok_spec = mod_spec_from(0), tok_spec_from(0)
    full = lambda shape: pl.BlockSpec(shape, lambda bi, ti: (0,) * len(shape))
    tok_shape = lambda w, dt: jax.ShapeDtypeStruct((b, s, w), dt)

    xs = jnp.concatenate([ctx, x], axis=1)
    last_ctx_reader = ((depth - 1) // 2) * 2

    for i in range(depth):
        j = i // 2
        t0 = 0 if i < last_ctx_reader else n_ct
        n_t = n_bt - t0
        mod_spec_o, tok_spec_o = mod_spec_from(t0), tok_spec_from(t0)
        route_out_specs = [tok_spec_o(d), tok_spec_o(d), tok_spec_o(LANES),
                           pl.BlockSpec((None, 1, LANES), lambda bi, ti, n_t=n_t: (bi * n_t + ti, 0, 0))]
        route_out_shapes = [tok_shape(d, F32), tok_shape(d, BF16), tok_shape(LANES, F32),
                            jax.ShapeDtypeStruct((b * n_t, 1, LANES), F32)]
        mods_l = mods[i]
        g1 = norm1_g[i].reshape(1, d)
        g2 = norm2_g[i].reshape(1, d)
        wr = jnp.zeros((d, LANES), F32).at[:, :n_exp].set(router_w[i])
        wr_hi = wr.astype(BF16)
        wr_lo = (wr - wr_hi.astype(F32)).astype(BF16)
        br = jnp.full((1, LANES), -1e30, F32).at[0, :n_exp].set(router_b[i])
        if i % 2 == 0:
            w_in = gdn_w_in[j]
            w_main = w_in[:, :qkv_w + qk_w].astype(BF16)
            n_ab = w_in.shape[1] - (qkv_w + qk_w)
            w_ab = jnp.zeros((d, LANES), F32).at[:, :n_ab].set(w_in[:, qkv_w + qk_w:]).astype(BF16)
            ab_c = jnp.zeros((8, LANES), F32)
            ab_c = ab_c.at[0, :2 * GDN_HEADS].set(-jnp.exp(gdn_a_log[j].reshape(-1)))
            ab_c = ab_c.at[1, :2 * GDN_HEADS].set(gdn_dt_bias[j].reshape(-1))
            qkv, z, gb = pl.pallas_call(
                functools.partial(_gdn_in_kernel, qkv_w=qkv_w),
                grid=(b, n_bt),
                in_specs=[tok_spec(d), mod_spec, full((1, d)), full((d, qkv_w + qk_w)), full((d, LANES)),
                          full((8, LANES))],
                out_specs=[tok_spec(qkv_w), tok_spec(qk_w), tok_spec(LANES)],
                out_shape=[tok_shape(qkv_w, BF16), tok_shape(qk_w, BF16), tok_shape(LANES, F32)],
                compiler_params=_cparams(("parallel", "parallel")),
                name="gdn_in",
            )(xs, mods_l, g1, w_main, w_ab, ab_c)

            qkv_c = pl.pallas_call(
                functools.partial(_gdn_conv_kernel, n_ctx=n_ctx, n_lat=t, qk_blocks=2 * GDN_HEADS),
                grid=(b, qkv_w // LANES),
                in_specs=[pl.BlockSpec((None, s, LANES), lambda bi, ci: (bi, 0, ci)),
                          pl.BlockSpec((GDN_CONV, LANES), lambda bi, ci: (0, ci))],
                out_specs=pl.BlockSpec((None, s, LANES), lambda bi, ci: (bi, 0, ci)),
                out_shape=tok_shape(qkv_w, BF16),
                scratch_shapes=[pltpu.VMEM((s + 24, LANES), F32)],
                compiler_params=_cparams(("parallel", "parallel")),
                name="gdn_conv",
            )(qkv, gdn_conv_w[j])

            gbt = jnp.swapaxes(gb[:, :, :4 * GDN_HEADS], 1, 2)

            def bwd_tile(ti):
                return jnp.where(ti < n_ct, n_ct - 1 - ti, n_bt - 1 - (ti - n_ct))

            hw = SCAN_HEADS * GDN_DK
            n_hb = GDN_HEADS // SCAN_HEADS

            def hd_spec(off, tmap):
                return pl.BlockSpec((None, TOK_TILE, hw), lambda bi, hi, ti: (bi, tmap(ti), off + hi))

            fwd_tile = lambda ti: ti
            scan_in = []
            for tmap in (fwd_tile, bwd_tile):
                scan_in += [hd_spec(0, tmap), hd_spec(n_hb, tmap), hd_spec(2 * n_hb, tmap),
                            pl.BlockSpec((None, TOK_TILE, LANES), lambda bi, hi, ti, tmap=tmap: (bi, tmap(ti), 0)),
                            pl.BlockSpec((None, 4 * GDN_HEADS, TOK_TILE),
                                         lambda bi, hi, ti, tmap=tmap: (bi, 0, tmap(ti)))]
            o_f, o_b = pl.pallas_call(
                functools.partial(_gdn_scan_kernel, scale=GDN_DK ** -0.5),
                grid=(b, n_hb, n_bt),
                in_specs=scan_in,
                out_specs=[hd_spec(0, fwd_tile), hd_spec(0, bwd_tile)],
                out_shape=[tok_shape(qk_w, BF16), tok_shape(qk_w, BF16)],
                scratch_shapes=[pltpu.VMEM((SCAN_HEADS, GDN_DK, GDN_DK), F32),
                                pltpu.VMEM((SCAN_HEADS, GDN_DK, GDN_DK), F32)],
                compiler_params=_cparams(("parallel", "parallel", "arbitrary")),
                name="gdn_scan",
            )(qkv_c, qkv_c, qkv_c, gb, gbt, qkv_c, qkv_c, qkv_c, gb, gbt)

            xs, h2, route, counts = pl.pallas_call(
                _gdn_out_kernel,
                grid=(b, n_t),
                in_specs=[tok_spec_o(qk_w), tok_spec_o(qk_w), tok_spec_o(qk_w), tok_spec_o(d), mod_spec_o,
                          full((1, LANES)), full((qk_w, d)), full((1, d)), full((d, LANES)), full((d, LANES)),
                          full((1, LANES))],
                out_specs=route_out_specs,
                out_shape=route_out_shapes,
                input_output_aliases={3: 0},
                compiler_params=_cparams(("parallel", "parallel")),
                name="gdn_out",
            )(o_f, o_b, z, xs, mods_l, gdn_norm_g[j].reshape(1, LANES), gdn_w_out[j].astype(BF16), g2,
              wr_hi, wr_lo, br)
        else:
            width = gmlp_w_out.shape[1]
            xs, h2, route, counts = pl.pallas_call(
                _gmlp_kernel,
                grid=(b, n_t),
                in_specs=[tok_spec_o(d), mod_spec_o, full((1, d)), full((d, 2 * width)), full((1, 2 * width)),
                          full((1, width)), full((1, width)), full((GMLP_GROUPS, GMLP_CHUNK, GMLP_CHUNK)),
                          full((GMLP_CHUNK, GMLP_GROUPS)), full((width, d)), full((1, d)), full((d, LANES)),
                          full((d, LANES)), full((1, LANES))],
                out_specs=route_out_specs,
                out_shape=route_out_shapes,
                input_output_aliases={0: 0},
                compiler_params=_cparams(("parallel", "parallel")),
                name="gmlp",
            )(xs, mods_l, g1, gmlp_w_in[j].astype(BF16), gmlp_b_in[j].reshape(1, -1),
              gmlp_ln_g[j].reshape(1, -1), gmlp_ln_b[j].reshape(1, -1), gmlp_w_s[j].astype(BF16),
              gmlp_b_s[j].T, gmlp_w_out[j].astype(BF16), g2, wr_hi, wr_lo, br)

        fuse_final = i == depth - 1 and t0 == n_ct
        xs = _moe(xs, h2, route, counts, mods_l, mod_map, i, moe_w_gu, moe_b_gu, moe_w_dn, moe_b_dn, n_bt, t0,
                  final_g if fuse_final else None)
        if fuse_final:
            return xs.astype(x.dtype)

    return pl.pallas_call(
        _final_kernel,
        grid=(b, t // TOK_TILE),
        in_specs=[pl.BlockSpec((None, TOK_TILE, d), lambda bi, ti: (bi, n_ct + ti, 0)), full((1, d))],
        out_specs=pl.BlockSpec((None, TOK_TILE, d), lambda bi, ti: (bi, ti, 0)),
        out_shape=jax.ShapeDtypeStruct((b, t, d), x.dtype),
        compiler_params=_cparams(("parallel", "parallel")),
        name="final_norm",
    )(xs, final_g.reshape(1, d))
```
